```python
import math
import jax, jax.numpy as jnp
from jax import lax
import numpy as np

D_MODEL = 2048
BATCH = 1
SEQ = 16384
DEPTH = 2

CHUNK = 64
Q_BLOCK = 128
MLA_HEADS = 8
Q_LORA = 512
KV_LORA = 256
NOPE_DIM = 128
ROPE_DIM = 64
QK_HEAD = NOPE_DIM + ROPE_DIM
MLA_V_DIM = 128
MLA_WIDTH = MLA_HEADS * MLA_V_DIM
ROPE_BASE = 10000.0
GLA_HEADS = 4
GLA_DK = 128
GLA_DV = 256
GLA_K_WIDTH = GLA_HEADS * GLA_DK
GLA_WIDTH = GLA_HEADS * GLA_DV
GATE_RANK = 16
GATE_TAU = 16.0
MIX_WIDTH = MLA_WIDTH + GLA_WIDTH
IN_SPLITS = (Q_LORA, KV_LORA, ROPE_DIM, GLA_K_WIDTH, GLA_K_WIDTH, GLA_WIDTH, GATE_RANK, GLA_WIDTH)
IN_DIM = sum(IN_SPLITS)
D_FF = 5632
CONV_W = 3
NORM_EPS = 1e-6

kernel_name = "hybrid_mla_gla_convffn_trunk"


def rms_norm(x, gain):
    x32 = x.astype(jnp.float32)
    y = x32 * lax.rsqrt(jnp.mean(x32 * x32, axis=-1, keepdims=True) + NORM_EPS)
    return (y * gain.astype(jnp.float32)).astype(x.dtype)


def split_cols(z, sizes):
    outs, start = [], 0
    for s in sizes:
        outs.append(z[..., start:start + s])
        start += s
    return outs


def apply_rope(x, cos, sin):
    x1, x2 = x[..., :ROPE_DIM // 2], x[..., ROPE_DIM // 2:]
    return jnp.concatenate([x1 * cos - x2 * sin, x2 * cos + x1 * sin], axis=-1)


def block_causal_attention(q, k, v):
    B, H, S, Dq = q.shape
    nb = S // Q_BLOCK
    scale = 1.0 / math.sqrt(Dq)
    qb = q.reshape(B, H, nb, Q_BLOCK, Dq).transpose(2, 0, 1, 3, 4)
    key_chunk = jnp.arange(S) // CHUNK

    def one_block(args):
        q_blk, b = args
        q_chunk = (b * Q_BLOCK + jnp.arange(Q_BLOCK)) // CHUNK
        s = jnp.einsum('bhqd,bhkd->bhqk', q_blk, k).astype(jnp.float32) * scale
        mask = key_chunk[None, :] <= q_chunk[:, None]
        s = jnp.where(mask[None, None], s, jnp.float32(-1e30))
        p = jax.nn.softmax(s, axis=-1)
        return jnp.einsum('bhqk,bhkd->bhqd', p.astype(v.dtype), v)

    out = lax.map(one_block, (qb, jnp.arange(nb)))
    return out.transpose(1, 2, 0, 3, 4).reshape(B, H, S, v.shape[-1])


def mla_mixer(z_cq, z_ckv, z_kr, cos, sin, cq_norm, w_uq, ckv_norm, w_ukv, q_norm, k_norm):
    B, S, _ = z_cq.shape
    q = (rms_norm(z_cq, cq_norm) @ w_uq).reshape(B, S, MLA_HEADS, QK_HEAD)
    kv = (rms_norm(z_ckv, ckv_norm) @ w_ukv).reshape(B, S, MLA_HEADS, NOPE_DIM + MLA_V_DIM)
    k_nope, v = kv[..., :NOPE_DIM], kv[..., NOPE_DIM:]
    k_rope = jnp.broadcast_to(z_kr[:, :, None, :], (B, S, MLA_HEADS, ROPE_DIM))
    k = jnp.concatenate([k_nope, k_rope], axis=-1)
    q = rms_norm(q, q_norm)
    k = rms_norm(k, k_norm)
    q = jnp.concatenate([q[..., :NOPE_DIM], apply_rope(q[..., NOPE_DIM:], cos, sin)], axis=-1)
    k = jnp.concatenate([k[..., :NOPE_DIM], apply_rope(k[..., NOPE_DIM:], cos, sin)], axis=-1)
    out = block_causal_attention(q.transpose(0, 2, 1, 3), k.transpose(0, 2, 1, 3), v.transpose(0, 2, 1, 3))
    return out.transpose(0, 2, 1, 3).reshape(B, S, MLA_WIDTH)


def gla_mixer(z_q, z_k, z_v, z_a, z_g, w_gate, b_gate, out_norm):
    B, S, _ = z_q.shape
    N = S // CHUNK

    def heads(t, d):
        return t.reshape(B, N, CHUNK, GLA_HEADS, d).transpose(0, 3, 1, 2, 4)

    q = heads(z_q, GLA_DK) * (GLA_DK ** -0.5)
    k = heads(z_k, GLA_DK)
    v = heads(z_v, GLA_DV)
    log_a = jax.nn.log_sigmoid((z_a @ w_gate + b_gate).astype(jnp.float32)) / GATE_TAU
    log_a = heads(log_a, GLA_DK)
    bcum = jnp.cumsum(log_a, axis=3)
    bend = bcum[:, :, :, -1:, :]
    k_dec = (k.astype(jnp.float32) * jnp.exp(bend - bcum)).astype(k.dtype)
    u = jnp.einsum('bhncd,bhnce->nbhde', k_dec, v)
    decay = jnp.exp(bend[:, :, :, 0, :]).transpose(2, 0, 1, 3)
    qn = q.transpose(2, 0, 1, 3, 4)

    def step(state, xs):
        u_c, d_c, q_c = xs
        state = state * d_c[..., None] + u_c.astype(jnp.float32)
        o = jnp.einsum('bhcd,bhde->bhce', q_c.astype(jnp.float32), state)
        return state, o

    init = jnp.zeros((B, GLA_HEADS, GLA_DK, GLA_DV), jnp.float32)
    _, o = lax.scan(step, init, (u, decay, qn))
    o = o.transpose(1, 0, 3, 2, 4).reshape(B, S, GLA_HEADS, GLA_DV).astype(z_v.dtype)
    o = rms_norm(o, out_norm).reshape(B, S, GLA_WIDTH)
    return o * jax.nn.silu(z_g)


def causal_dwconv(u, w, b):
    S = u.shape[1]
    up = jnp.pad(u, ((0, 0), (CONV_W - 1, 0), (0, 0)))
    y = b
    for j in range(CONV_W):
        y = y + up[:, j:j + S, :] * w[j]
    return y


def setup_inputs(seed: int = 0) -> dict:
    key = jax.random.key(seed)
    ks = jax.random.split(key, 24)
    L = DEPTH

    def nrm(k, shape, fan_in):
        return jax.random.normal(k, shape, jnp.float32) * (fan_in ** -0.5)

    def gain(k, shape):
        return 1.0 + 0.02 * jax.random.normal(k, shape, jnp.float32)

    x = jax.random.normal(ks[0], (BATCH, SEQ, D_MODEL), jnp.float32)
    positions = jnp.broadcast_to(jnp.arange(SEQ, dtype=jnp.int32), (BATCH, SEQ))
    return {
        "x": x,
        "positions": positions,
        "attn_norm": gain(ks[1], (L, D_MODEL)),
        "w_in": nrm(ks[2], (L, D_MODEL, IN_DIM), D_MODEL),
        "cq_norm": gain(ks[3], (L, Q_LORA)),
        "w_uq": nrm(ks[4], (L, Q_LORA, MLA_HEADS * QK_HEAD), Q_LORA),
        "ckv_norm": gain(ks[5], (L, KV_LORA)),
        "w_ukv": nrm(ks[6], (L, KV_LORA, MLA_HEADS * (NOPE_DIM + MLA_V_DIM)), KV_LORA),
        "q_norm": gain(ks[7], (L, QK_HEAD)),
        "k_norm": gain(ks[8], (L, QK_HEAD)),
        "w_gate": nrm(ks[9], (L, GATE_RANK, GLA_K_WIDTH), GATE_RANK),
        "b_gate": 0.1 * jax.random.normal(ks[10], (L, GLA_K_WIDTH), jnp.float32),
        "gla_out_norm": gain(ks[11], (L, GLA_DV)),
        "mla_out_norm": gain(ks[12], (L, MLA_WIDTH)),
        "w_out": nrm(ks[13], (L, MIX_WIDTH, D_MODEL), MIX_WIDTH),
        "ffn_norm": gain(ks[14], (L, D_MODEL)),
        "w_up": nrm(ks[15], (L, D_MODEL, 2 * D_FF), D_MODEL),
        "conv_w": nrm(ks[16], (L, CONV_W, 2 * D_FF), CONV_W),
        "conv_b": 0.02 * jax.random.normal(ks[17], (L, 2 * D_FF), jnp.float32),
        "w_down": nrm(ks[18], (L, D_FF, D_MODEL), D_FF),
    }


def reference(x, positions, attn_norm, w_in, cq_norm, w_uq, ckv_norm, w_ukv, q_norm, k_norm,
              w_gate, b_gate, gla_out_norm, mla_out_norm, w_out, ffn_norm, w_up, conv_w, conv_b, w_down):
    inv_freq = ROPE_BASE ** (-jnp.arange(0, ROPE_DIM, 2, dtype=jnp.float32) / ROPE_DIM)
    angle = positions.astype(jnp.float32)[..., None] * inv_freq
    cos = jnp.cos(angle)[:, :, None, :].astype(x.dtype)
    sin = jnp.sin(angle)[:, :, None, :].astype(x.dtype)
    for l in range(DEPTH):
        h = rms_norm(x, attn_norm[l])
        z = h @ w_in[l]
        z_cq, z_ckv, z_kr, z_q, z_k, z_v, z_a, z_g = split_cols(z, IN_SPLITS)
        a_out = mla_mixer(z_cq, z_ckv, z_kr, cos, sin, cq_norm[l], w_uq[l], ckv_norm[l], w_ukv[l],
                          q_norm[l], k_norm[l])
        a_out = rms_norm(a_out, mla_out_norm[l])
        g_out = gla_mixer(z_q, z_k, z_v, z_a, z_g, w_gate[l], b_gate[l], gla_out_norm[l])
        x = x + jnp.concatenate([a_out, g_out], axis=-1) @ w_out[l]
        h = rms_norm(x, ffn_norm[l])
        u = causal_dwconv(h @ w_up[l], conv_w[l], conv_b[l])
        gate, up = u[..., :D_FF], u[..., D_FF:]
        x = x + (jax.nn.silu(gate) * up) @ w_down[l]
    return x
```

```python
import functools
import math

import jax
import jax.numpy as jnp
from jax import lax
from jax.experimental import pallas as pl
from jax.experimental.pallas import tpu as pltpu

F32 = jnp.float32
BF16 = jnp.bfloat16

D_MODEL = 2048
DEPTH = 2
CHUNK = 64
MLA_HEADS = 8
Q_LORA = 512
KV_LORA = 256
NOPE_DIM = 128
ROPE_DIM = 64
ROPE_HALF = ROPE_DIM // 2
QK_HEAD = NOPE_DIM + ROPE_DIM
MLA_V_DIM = 128
MLA_WIDTH = MLA_HEADS * MLA_V_DIM
ROPE_BASE = 10000.0
GLA_HEADS = 4
GLA_DK = 128
GLA_DV = 256
GLA_K_WIDTH = GLA_HEADS * GLA_DK
GLA_WIDTH = GLA_HEADS * GLA_DV
GATE_RANK = 16
GATE_TAU = 16.0
MIX_WIDTH = MLA_WIDTH + GLA_WIDTH
D_FF = 5632
CONV_W = 3
NORM_EPS = 1e-6

LANES = 128
HEAD_PAD = 2 * LANES

Z_CQ, Z_Q, Z_K, Z_CKV, Z_KRA, Z_V, Z_G = 0, 512, 1024, 1536, 1792, 2048, 3072
Z_WIDTH = 4096
KRA_A_OFF = ROPE_HALF
KRA_X2_OFF = 2 * ROPE_HALF

VMEM_LIMIT = 56 * 1024 * 1024


def _cparams(sem):
    return pltpu.CompilerParams(dimension_semantics=sem, vmem_limit_bytes=VMEM_LIMIT)


def _rms(x, gain):
    ms = jnp.mean(x * x, axis=-1, keepdims=True)
    return x * lax.rsqrt(ms + NORM_EPS) * gain


def _in_proj_kernel(x_ref, g_ref, w_ref, o_ref, hn_ref):
    @pl.when(pl.program_id(1) == 0)
    def _():
        hn_ref[...] = _rms(x_ref[...], g_ref[...]).astype(BF16)

    o_ref[...] = jnp.dot(hn_ref[...], w_ref[...], preferred_element_type=F32).astype(o_ref.dtype)


def _in_proj(x, gain, w, tm, tn):
    s, d = x.shape
    n = w.shape[1]
    return pl.pallas_call(
        _in_proj_kernel,
        grid=(s // tm, n // tn),
        in_specs=[
            pl.BlockSpec((tm, d), lambda i, j: (i, 0)),
            pl.BlockSpec((1, d), lambda i, j: (0, 0)),
            pl.BlockSpec((d, tn), lambda i, j: (0, j)),
        ],
        out_specs=pl.BlockSpec((tm, tn), lambda i, j: (i, j)),
        out_shape=jax.ShapeDtypeStruct((s, n), BF16),
        scratch_shapes=[pltpu.VMEM((tm, d), BF16)],
        compiler_params=_cparams(("arbitrary", "arbitrary")),
        name="in_proj",
    )(x, gain, w)


def _rope(b, ct, st):
    return b * ct + pltpu.roll(b, KRA_X2_OFF, 1) * st


def _mla_prep_kernel(cq_ref, ckv_ref, kra_ref, ct_ref, st_ref, cqg_ref, ckvg_ref, wuq_ref,
                     wukv_ref, qg_ref, kg_ref, q_ref, k_ref, v_ref):
    ct = ct_ref[...]
    st = st_ref[...]
    inv_head = 1.0 / QK_HEAD

    cqn = _rms(cq_ref[...].astype(F32), cqg_ref[...]).astype(BF16)
    qall = jnp.dot(cqn, wuq_ref[...], preferred_element_type=F32)
    qg = qg_ref[...]
    for h in range(MLA_HEADS):
        qh = qall[:, h * HEAD_PAD:(h + 1) * HEAD_PAD]
        ssq = jnp.sum(qh * qh, axis=-1, keepdims=True)
        qh = qh * lax.rsqrt(ssq * inv_head + NORM_EPS) * qg
        q_ref[:, h * HEAD_PAD:h * HEAD_PAD + LANES] = qh[:, :LANES].astype(BF16)
        q_ref[:, h * HEAD_PAD + LANES:(h + 1) * HEAD_PAD] = _rope(qh[:, LANES:], ct, st).astype(BF16)

    ckvn = _rms(ckv_ref[...].astype(F32), ckvg_ref[...]).astype(BF16)
    kv = jnp.dot(ckvn, wukv_ref[...], preferred_element_type=F32)
    kg = kg_ref[...]
    kg_n = kg[:, :LANES]
    kg_r = kg[:, LANES:]
    kr = kra_ref[...].astype(F32)
    lane = lax.broadcasted_iota(jnp.int32, kr.shape, 1)
    is_rope = (lane < ROPE_HALF) | ((lane >= KRA_X2_OFF) & (lane < KRA_X2_OFF + ROPE_HALF))
    kr = jnp.where(is_rope, kr, 0.0)
    ssq_r = jnp.sum(kr * kr, axis=-1, keepdims=True)
    kr_rot = _rope(kr * kg_r, ct, st)
    for h in range(MLA_HEADS):
        kn = kv[:, h * NOPE_DIM:(h + 1) * NOPE_DIM]
        ssq = jnp.sum(kn * kn, axis=-1, keepdims=True) + ssq_r
        r = lax.rsqrt(ssq * inv_head + NORM_EPS)
        k_ref[:, h * HEAD_PAD:h * HEAD_PAD + LANES] = (kn * r * kg_n).astype(BF16)
        k_ref[:, h * HEAD_PAD + LANES:(h + 1) * HEAD_PAD] = (kr_rot * r).astype(BF16)
    v_ref[...] = kv[:, MLA_HEADS * NOPE_DIM:].astype(BF16)


def _mla_prep(z, ct, st, cqg, ckvg, wuq, wukv, qg, kg, tm):
    s = z.shape[0]
    full = lambda a: pl.BlockSpec(a.shape, lambda i: (0, 0))
    return pl.pallas_call(
        _mla_prep_kernel,
        grid=(s // tm,),
        in_specs=[
            pl.BlockSpec((tm, Q_LORA), lambda i: (i, Z_CQ // Q_LORA)),
            pl.BlockSpec((tm, KV_LORA), lambda i: (i, Z_CKV // KV_LORA)),
            pl.BlockSpec((tm, LANES), lambda i: (i, Z_KRA // LANES)),
            pl.BlockSpec((tm, LANES), lambda i: (i, 0)),
            pl.BlockSpec((tm, LANES), lambda i: (i, 0)),
            full(cqg), full(ckvg), full(wuq), full(wukv), full(qg), full(kg),
        ],
        out_specs=[
            pl.BlockSpec((tm, MLA_HEADS * HEAD_PAD), lambda i: (i, 0)),
            pl.BlockSpec((tm, MLA_HEADS * HEAD_PAD), lambda i: (i, 0)),
            pl.BlockSpec((tm, MLA_WIDTH), lambda i: (i, 0)),
        ],
        out_shape=[
            jax.ShapeDtypeStruct((s, MLA_HEADS * HEAD_PAD), BF16),
            jax.ShapeDtypeStruct((s, MLA_HEADS * HEAD_PAD), BF16),
            jax.ShapeDtypeStruct((s, MLA_WIDTH), BF16),
        ],
        compiler_params=_cparams(("arbitrary",)),
        name="mla_prep",
    )(z, z, z, ct, st, cqg, ckvg, wuq, wukv, qg, kg)


def _attn_kernel(q_ref, k_ref, v_ref, o_ref, m_ref, l_ref, acc_ref, *, tq):
    qi = pl.program_id(1)
    q = q_ref[...]
    m_ref[...] = jnp.full(m_ref.shape, -1e30, F32)
    l_ref[...] = jnp.zeros(l_ref.shape, F32)
    acc_ref[...] = jnp.zeros(acc_ref.shape, F32)

    def tile(kj, masked):
        start = pl.multiple_of(kj * tq, tq)
        k = k_ref[pl.ds(start, tq), :]
        v = v_ref[pl.ds(start, tq), :]
        s = lax.dot_general(q, k, (((1,), (1,)), ((), ())), preferred_element_type=F32)
        if masked:
            qc = lax.broadcasted_iota(jnp.int32, s.shape, 0) // CHUNK
            kc = lax.broadcasted_iota(jnp.int32, s.shape, 1) // CHUNK
            s = jnp.where(kc <= qc, s, -1e30)
        m_old = m_ref[...]
        m_new = jnp.maximum(m_old, jnp.max(s, axis=-1, keepdims=True))
        alpha = jnp.exp(m_old - m_new)
        p = jnp.exp(s - m_new)
        l_ref[...] = alpha * l_ref[...] + jnp.sum(p, axis=-1, keepdims=True)
        acc_ref[...] = alpha * acc_ref[...] + jnp.dot(p.astype(BF16), v, preferred_element_type=F32)
        m_ref[...] = m_new

    def body(kj, carry):
        tile(kj, False)
        return carry

    lax.fori_loop(0, qi, body, 0)
    tile(qi, True)
    o_ref[...] = (acc_ref[...] / l_ref[...]).astype(o_ref.dtype)


def _attention(q, k, v, tq):
    s = q.shape[0]
    return pl.pallas_call(
        functools.partial(_attn_kernel, tq=tq),
        grid=(MLA_HEADS, s // tq),
        in_specs=[
            pl.BlockSpec((tq, HEAD_PAD), lambda h, i: (i, h)),
            pl.BlockSpec((s, HEAD_PAD), lambda h, i: (0, h)),
            pl.BlockSpec((s, MLA_V_DIM), lambda h, i: (0, h)),
        ],
        out_specs=pl.BlockSpec((tq, MLA_V_DIM), lambda h, i: (i, h)),
        out_shape=jax.ShapeDtypeStruct((s, MLA_WIDTH), BF16),
        scratch_shapes=[
            pltpu.VMEM((tq, 1), F32),
            pltpu.VMEM((tq, 1), F32),
            pltpu.VMEM((tq, MLA_V_DIM), F32),
        ],
        compiler_params=_cparams(("arbitrary", "arbitrary")),
        name="mla_attention",
    )(q, k, v)


def _gla_kernel(q_ref, k_ref, v_ref, g_ref, kra_ref, wg_ref, bg_ref, on_ref, tri_ref, o_ref,
                st_ref, *, tm):
    @pl.when(pl.program_id(0) == 0)
    def _():
        st_ref[...] = jnp.zeros(st_ref.shape, F32)

    pre = jnp.dot(kra_ref[...], wg_ref[...], preferred_element_type=F32) + bg_ref[...]
    log_a = (jnp.minimum(pre, 0.0) - jnp.log1p(jnp.exp(-jnp.abs(pre)))) * (1.0 / GATE_TAU)
    hi = log_a.astype(BF16)
    lo = (log_a - hi.astype(F32)).astype(BF16)
    tri = tri_ref[...]
    suffix = (jnp.dot(tri, hi, preferred_element_type=F32)
              + jnp.dot(tri, lo, preferred_element_type=F32))
    kd = k_ref[...].astype(F32) * jnp.exp(suffix)
    row = lax.broadcasted_iota(jnp.int32, kd.shape, 0)
    first_half = ((row // CHUNK) % 2) == 0
    kd_pair = (jnp.where(first_half, kd, 0.0).astype(BF16),
               jnp.where(first_half, 0.0, kd).astype(BF16))
    vt = v_ref[...].astype(F32).T.astype(BF16)
    on = on_ref[...]
    scale = GLA_DK ** -0.5

    for c in range(tm // CHUNK):
        r0 = c * CHUNK
        p0 = (c // 2) * 2 * CHUNK
        decay = jnp.exp(jnp.sum(log_a[r0:r0 + CHUNK, :], axis=0, keepdims=True))
        for h in range(GLA_HEADS):
            kcols = slice(h * GLA_DK, (h + 1) * GLA_DK)
            vcols = slice(h * GLA_DV, (h + 1) * GLA_DV)
            ut = jnp.dot(vt[vcols, p0:p0 + 2 * CHUNK], kd_pair[c % 2][p0:p0 + 2 * CHUNK, kcols],
                         preferred_element_type=F32)
            state = st_ref[h] * decay[:, kcols] + ut
            st_ref[h] = state
            o = lax.dot_general(q_ref[r0:r0 + CHUNK, kcols], state.astype(BF16),
                                (((1,), (1,)), ((), ())), preferred_element_type=F32) * scale
            o = _rms(o, on)
            gate = g_ref[r0:r0 + CHUNK, vcols].astype(F32)
            o_ref[r0:r0 + CHUNK, vcols] = (o * (gate * jax.nn.sigmoid(gate))).astype(o_ref.dtype)


def _gla(z, wg, bg, on, tri, tm):
    s = z.shape[0]
    full = lambda a: pl.BlockSpec(a.shape, lambda i: (0, 0))
    return pl.pallas_call(
        functools.partial(_gla_kernel, tm=tm),
        grid=(s // tm,),
        in_specs=[
            pl.BlockSpec((tm, GLA_K_WIDTH), lambda i: (i, Z_Q // GLA_K_WIDTH)),
            pl.BlockSpec((tm, GLA_K_WIDTH), lambda i: (i, Z_K // GLA_K_WIDTH)),
            pl.BlockSpec((tm, GLA_WIDTH), lambda i: (i, Z_V // GLA_WIDTH)),
            pl.BlockSpec((tm, GLA_WIDTH), lambda i: (i, Z_G // GLA_WIDTH)),
            pl.BlockSpec((tm, LANES), lambda i: (i, Z_KRA // LANES)),
            full(wg), full(bg), full(on), full(tri),
        ],
        out_specs=pl.BlockSpec((tm, GLA_WIDTH), lambda i: (i, 0)),
        out_shape=jax.ShapeDtypeStruct((s, GLA_WIDTH), BF16),
        scratch_shapes=[pltpu.VMEM((GLA_HEADS, GLA_DV, GLA_DK), F32)],
        compiler_params=_cparams(("arbitrary",)),
        name="gla",
    )(z, z, z, z, z, wg, bg, on, tri)


def _out_proj_kernel(a_ref, g_ref, x_ref, an_ref, w_ref, o_ref):
    an = _rms(a_ref[...].astype(F32), an_ref[...]).astype(BF16)
    acc = jnp.dot(an, w_ref[:MLA_WIDTH, :], preferred_element_type=F32)
    acc = acc + jnp.dot(g_ref[...], w_ref[MLA_WIDTH:, :], preferred_element_type=F32)
    o_ref[...] = x_ref[...] + acc


def _out_proj(a, g, x, an, w, tm):
    s, d = x.shape
    return pl.pallas_call(
        _out_proj_kernel,
        grid=(s // tm,),
        in_specs=[
            pl.BlockSpec((tm, MLA_WIDTH), lambda i: (i, 0)),
            pl.BlockSpec((tm, GLA_WIDTH), lambda i: (i, 0)),
            pl.BlockSpec((tm, d), lambda i: (i, 0)),
            pl.BlockSpec((1, MLA_WIDTH), lambda i: (0, 0)),
            pl.BlockSpec((MIX_WIDTH, d), lambda i: (0, 0)),
        ],
        out_specs=pl.BlockSpec((tm, d), lambda i: (i, 0)),
        out_shape=jax.ShapeDtypeStruct((s, d), F32),
        compiler_params=_cparams(("arbitrary",)),
        name="out_proj",
    )(a, g, x, an, w)


HALO = 8
FIX = 16


def _ffn_kernel(x_ref, g_ref, wg_ref, wu_ref, cwg_ref, cwu_ref, cbg_ref, cbu_ref, wd_ref, o_ref,
                hn_ref, carry_ref, act_ref, *, tm):
    i = pl.program_id(0)
    f = pl.program_id(1)

    @pl.when(f == 0)
    def _():
        x = x_ref[...]
        hn_ref[...] = _rms(x, g_ref[...]).astype(BF16)
        o_ref[...] = x

    @pl.when(i == 0)
    def _():
        carry_ref[f] = jnp.zeros(carry_ref.shape[1:], F32)

    hn = hn_ref[...]

    def conv(u, cw_ref, cb_ref, slot):
        w0, w1, w2 = cw_ref[0:1, :], cw_ref[1:2, :], cw_ref[2:3, :]
        cb = cb_ref[...]
        y = cb + w0 * pltpu.roll(u, 2, 0) + w1 * pltpu.roll(u, 1, 0) + w2 * u
        ext = jnp.concatenate([carry_ref[f, slot], u[:FIX, :]], axis=0)
        y_fix = (cb + w0 * pltpu.roll(ext, 2, 0)[HALO:, :] + w1 * pltpu.roll(ext, 1, 0)[HALO:, :]
                 + w2 * ext[HALO:, :])
        carry_ref[f, slot] = u[tm - HALO:, :]
        return y, y_fix

    ug = jnp.dot(hn, wg_ref[...], preferred_element_type=F32)
    yg, yg_fix = conv(ug, cwg_ref, cbg_ref, 0)
    uu = jnp.dot(hn, wu_ref[...], preferred_element_type=F32)
    yu, yu_fix = conv(uu, cwu_ref, cbu_ref, 1)
    act_ref[...] = (yg * jax.nn.sigmoid(yg) * yu).astype(BF16)
    act_ref[0:FIX, :] = (yg_fix * jax.nn.sigmoid(yg_fix) * yu_fix).astype(BF16)
    o_ref[...] += jnp.dot(act_ref[...], wd_ref[...], preferred_element_type=F32)


def _ffn(x, gain, w_up, conv_w, conv_b, w_down, tm, tf):
    s, d = x.shape
    nf = D_FF // tf
    return pl.pallas_call(
        functools.partial(_ffn_kernel, tm=tm),
        grid=(s // tm, nf),
        in_specs=[
            pl.BlockSpec((tm, d), lambda i, f: (i, 0)),
            pl.BlockSpec((1, d), lambda i, f: (0, 0)),
            pl.BlockSpec((d, tf), lambda i, f: (0, f)),
            pl.BlockSpec((d, tf), lambda i, f: (0, f + nf)),
            pl.BlockSpec((CONV_W, tf), lambda i, f: (0, f)),
            pl.BlockSpec((CONV_W, tf), lambda i, f: (0, f + nf)),
            pl.BlockSpec((1, tf), lambda i, f: (0, f)),
            pl.BlockSpec((1, tf), lambda i, f: (0, f + nf)),
            pl.BlockSpec((tf, d), lambda i, f: (f, 0)),
        ],
        out_specs=pl.BlockSpec((tm, d), lambda i, f: (i, 0)),
        out_shape=jax.ShapeDtypeStruct((s, d), F32),
        scratch_shapes=[
            pltpu.VMEM((tm, d), BF16),
            pltpu.VMEM((nf, 2, HALO, tf), F32),
            pltpu.VMEM((tm, tf), BF16),
        ],
        compiler_params=_cparams(("arbitrary", "arbitrary")),
        name="conv_ffn",
    )(x, gain, w_up, w_up, conv_w, conv_w, conv_b, conv_b, w_down)


def _pack_w_in(w):
    d = w.shape[0]
    o_cq, o_ckv, o_kr = 0, Q_LORA, Q_LORA + KV_LORA
    o_q = o_kr + ROPE_DIM
    o_k = o_q + GLA_K_WIDTH
    o_v = o_k + GLA_K_WIDTH
    o_a = o_v + GLA_WIDTH
    o_g = o_a + GATE_RANK
    zeros = lambda n: jnp.zeros((d, n), w.dtype)
    kra = jnp.concatenate([
        w[:, o_kr:o_kr + ROPE_HALF], w[:, o_a:o_a + GATE_RANK], zeros(KRA_X2_OFF - KRA_A_OFF - GATE_RANK),
        w[:, o_kr + ROPE_HALF:o_kr + ROPE_DIM], zeros(LANES - KRA_X2_OFF - ROPE_HALF)], axis=1)
    packed = jnp.concatenate([
        w[:, o_cq:o_cq + Q_LORA], w[:, o_q:o_q + GLA_K_WIDTH], w[:, o_k:o_k + GLA_K_WIDTH],
        w[:, o_ckv:o_ckv + KV_LORA], kra, zeros(Z_V - Z_KRA - LANES),
        w[:, o_v:o_v + GLA_WIDTH], w[:, o_g:o_g + GLA_WIDTH]], axis=1)
    return packed.astype(BF16)


def _pack_head_vec(g):
    z = jnp.zeros((ROPE_HALF,), g.dtype)
    return jnp.concatenate([g[:NOPE_DIM], g[NOPE_DIM:NOPE_DIM + ROPE_HALF], z,
                            g[NOPE_DIM + ROPE_HALF:], z])[None, :]


def _pack_w_uq(w):
    r = w.shape[0]
    w = w.reshape(r, MLA_HEADS, QK_HEAD)
    z = jnp.zeros((r, MLA_HEADS, ROPE_HALF), w.dtype)
    w = jnp.concatenate([w[..., :NOPE_DIM], w[..., NOPE_DIM:NOPE_DIM + ROPE_HALF], z,
                         w[..., NOPE_DIM + ROPE_HALF:], z], axis=-1)
    return w.reshape(r, MLA_HEADS * HEAD_PAD).astype(BF16)


def _pack_w_ukv(w):
    r = w.shape[0]
    w = w.reshape(r, MLA_HEADS, NOPE_DIM + MLA_V_DIM)
    return jnp.concatenate([w[..., :NOPE_DIM].reshape(r, -1), w[..., NOPE_DIM:].reshape(r, -1)],
                           axis=1).astype(BF16)


def _pack_w_gate(w):
    z = lambda n: jnp.zeros((n, w.shape[1]), w.dtype)
    return jnp.concatenate([z(KRA_A_OFF), w, z(LANES - KRA_A_OFF - GATE_RANK)], axis=0).astype(BF16)


def _rope_tables(positions):
    inv_freq = ROPE_BASE ** (-jnp.arange(0, ROPE_DIM, 2, dtype=F32) / ROPE_DIM)
    angle = positions.astype(F32)[:, None] * inv_freq
    cos, sin = jnp.cos(angle), jnp.sin(angle)
    z = jnp.zeros_like(cos)
    return (jnp.concatenate([cos, z, cos, z], axis=1), jnp.concatenate([-sin, z, sin, z], axis=1))


def _suffix_matrix(tm):
    t = jnp.arange(tm)
    same = (t[:, None] // CHUNK) == (t[None, :] // CHUNK)
    return (same & (t[None, :] > t[:, None])).astype(BF16)


def kernel(x, positions, attn_norm, w_in, cq_norm, w_uq, ckv_norm, w_ukv, q_norm, k_norm, w_gate,
           b_gate, gla_out_norm, mla_out_norm, w_out, ffn_norm, w_up, conv_w, conv_b, w_down):
    batch, seq, d = x.shape
    assert batch == 1 and d == D_MODEL
    tm_proj = min(1024, seq)
    tm = min(512, seq)
    tm_gla = min(256, seq)
    ct, st = _rope_tables(positions[0])
    tri = _suffix_matrix(tm_gla)
    q_scale = 1.0 / math.sqrt(QK_HEAD)
    h = x[0]
    for l in range(DEPTH):
        z = _in_proj(h, attn_norm[l][None, :], _pack_w_in(w_in[l]), tm_proj, 1024)
        q, k, v = _mla_prep(z, ct, st, cq_norm[l][None, :], ckv_norm[l][None, :],
                            _pack_w_uq(w_uq[l]), _pack_w_ukv(w_ukv[l]),
                            _pack_head_vec(q_norm[l]) * q_scale, _pack_head_vec(k_norm[l]), tm)
        a = _attention(q, k, v, tm)
        g = _gla(z, _pack_w_gate(w_gate[l]), b_gate[l][None, :], gla_out_norm[l][None, :], tri,
                 tm_gla)
        h = _out_proj(a, g, h, mla_out_norm[l][None, :], w_out[l].astype(BF16), tm)
        h = _ffn(h, ffn_norm[l][None, :], w_up[l].astype(BF16), conv_w[l], conv_b[l][None, :],
                 w_down[l].astype(BF16), tm, 512)
    return h[None]
```

```python
import functools
import math

import jax
import jax.numpy as jnp
from jax import lax
from jax.experimental import pallas as pl
from jax.experimental.pallas import tpu as pltpu

F32 = jnp.float32
BF16 = jnp.bfloat16

D_MODEL = 2048
DEPTH = 2
CHUNK = 64
MLA_HEADS = 8
Q_LORA = 512
KV_LORA = 256
NOPE_DIM = 128
ROPE_DIM = 64
ROPE_HALF = ROPE_DIM // 2
QK_HEAD = NOPE_DIM + ROPE_DIM
MLA_V_DIM = 128
MLA_WIDTH = MLA_HEADS * MLA_V_DIM
ROPE_BASE = 10000.0
GLA_HEADS = 4
GLA_DK = 128
GLA_DV = 256
GLA_K_WIDTH = GLA_HEADS * GLA_DK
GLA_WIDTH = GLA_HEADS * GLA_DV
GATE_RANK = 16
GATE_TAU = 16.0
MIX_WIDTH = MLA_WIDTH + GLA_WIDTH
D_FF = 5632
CONV_W = 3
NORM_EPS = 1e-6

LANES = 128
HEAD_PAD = 2 * LANES

Z_CQ, Z_Q, Z_K, Z_CKV, Z_KRA, Z_V, Z_G = 0, 512, 1024, 1536, 1792, 2048, 3072
Z_WIDTH = 4096
KRA_A_OFF = ROPE_HALF
KRA_X2_OFF = 2 * ROPE_HALF

VMEM_LIMIT = 56 * 1024 * 1024


def _cparams(sem):
    return pltpu.CompilerParams(dimension_semantics=sem, vmem_limit_bytes=VMEM_LIMIT)


def _rms(x, gain):
    ms = jnp.mean(x * x, axis=-1, keepdims=True)
    return x * lax.rsqrt(ms + NORM_EPS) * gain


def _in_proj_kernel(x_ref, g_ref, w_ref, o_ref, hn_ref):
    @pl.when(pl.program_id(1) == 0)
    def _():
        hn_ref[...] = _rms(x_ref[...], g_ref[...]).astype(BF16)

    o_ref[...] = jnp.dot(hn_ref[...], w_ref[...], preferred_element_type=F32).astype(o_ref.dtype)


def _in_proj(x, gain, w, tm, tn):
    s, d = x.shape
    n = w.shape[1]
    return pl.pallas_call(
        _in_proj_kernel,
        grid=(s // tm, n // tn),
        in_specs=[
            pl.BlockSpec((tm, d), lambda i, j: (i, 0)),
            pl.BlockSpec((1, d), lambda i, j: (0, 0)),
            pl.BlockSpec((d, tn), lambda i, j: (0, j)),
        ],
        out_specs=pl.BlockSpec((tm, tn), lambda i, j: (i, j)),
        out_shape=jax.ShapeDtypeStruct((s, n), BF16),
        scratch_shapes=[pltpu.VMEM((tm, d), BF16)],
        compiler_params=_cparams(("arbitrary", "arbitrary")),
        name="in_proj",
    )(x, gain, w)


def _rope(b, ct, st):
    return b * ct + pltpu.roll(b, KRA_X2_OFF, 1) * st


def _mla_prep_kernel(cq_ref, ckv_ref, kra_ref, ct_ref, st_ref, cqg_ref, ckvg_ref, wuq_ref,
                     wukv_ref, qg_ref, kg_ref, qt_ref, k_ref, vt_ref):
    ct = ct_ref[...]
    st = st_ref[...]
    inv_head = 1.0 / QK_HEAD

    cqn = _rms(cq_ref[...].astype(F32), cqg_ref[...]).astype(BF16)
    qall = jnp.dot(cqn, wuq_ref[...], preferred_element_type=F32)
    qg = qg_ref[...]
    for h in range(MLA_HEADS):
        qh = qall[:, h * HEAD_PAD:(h + 1) * HEAD_PAD]
        ssq = jnp.sum(qh * qh, axis=-1, keepdims=True)
        qh = qh * lax.rsqrt(ssq * inv_head + NORM_EPS) * qg
        qt_ref[h, :LANES, :] = qh[:, :LANES].T.astype(BF16)
        qt_ref[h, LANES:, :] = _rope(qh[:, LANES:], ct, st).T.astype(BF16)

    ckvn = _rms(ckv_ref[...].astype(F32), ckvg_ref[...]).astype(BF16)
    kv = jnp.dot(ckvn, wukv_ref[...], preferred_element_type=F32)
    kg = kg_ref[...]
    kg_n = kg[:, :LANES]
    kg_r = kg[:, LANES:]
    kr = kra_ref[...].astype(F32)
    lane = lax.broadcasted_iota(jnp.int32, kr.shape, 1)
    is_rope = (lane < ROPE_HALF) | ((lane >= KRA_X2_OFF) & (lane < KRA_X2_OFF + ROPE_HALF))
    kr = jnp.where(is_rope, kr, 0.0)
    ssq_r = jnp.sum(kr * kr, axis=-1, keepdims=True)
    kr_rot = _rope(kr * kg_r, ct, st)
    for h in range(MLA_HEADS):
        kn = kv[:, h * NOPE_DIM:(h + 1) * NOPE_DIM]
        ssq = jnp.sum(kn * kn, axis=-1, keepdims=True) + ssq_r
        r = lax.rsqrt(ssq * inv_head + NORM_EPS)
        k_ref[:, h * HEAD_PAD:h * HEAD_PAD + LANES] = (kn * r * kg_n).astype(BF16)
        k_ref[:, h * HEAD_PAD + LANES:(h + 1) * HEAD_PAD] = (kr_rot * r).astype(BF16)
        v0 = MLA_HEADS * NOPE_DIM + h * MLA_V_DIM
        vt_ref[h, 0] = kv[:, v0:v0 + MLA_V_DIM].T.astype(BF16)


def _mla_prep(z, ct, st, cqg, ckvg, wuq, wukv, qg, kg, tm):
    s = z.shape[0]
    full = lambda a: pl.BlockSpec(a.shape, lambda i: (0, 0))
    return pl.pallas_call(
        _mla_prep_kernel,
        grid=(s // tm,),
        in_specs=[
            pl.BlockSpec((tm, Q_LORA), lambda i: (i, Z_CQ // Q_LORA)),
            pl.BlockSpec((tm, KV_LORA), lambda i: (i, Z_CKV // KV_LORA)),
            pl.BlockSpec((tm, LANES), lambda i: (i, Z_KRA // LANES)),
            pl.BlockSpec((tm, LANES), lambda i: (i, 0)),
            pl.BlockSpec((tm, LANES), lambda i: (i, 0)),
            full(cqg), full(ckvg), full(wuq), full(wukv), full(qg), full(kg),
        ],
        out_specs=[
            pl.BlockSpec((MLA_HEADS, HEAD_PAD, tm), lambda i: (0, 0, i)),
            pl.BlockSpec((tm, MLA_HEADS * HEAD_PAD), lambda i: (i, 0)),
            pl.BlockSpec((MLA_HEADS, 1, MLA_V_DIM, tm), lambda i: (0, i, 0, 0)),
        ],
        out_shape=[
            jax.ShapeDtypeStruct((MLA_HEADS, HEAD_PAD, s), BF16),
            jax.ShapeDtypeStruct((s, MLA_HEADS * HEAD_PAD), BF16),
            jax.ShapeDtypeStruct((MLA_HEADS, s // tm, MLA_V_DIM, tm), BF16),
        ],
        compiler_params=_cparams(("arbitrary",)),
        name="mla_prep",
    )(z, z, z, ct, st, cqg, ckvg, wuq, wukv, qg, kg)


def _attn_kernel(qt_ref, k_ref, vt_ref, o_ref, sa_ref, sb_ref, m_ref, l_ref, acc_ref, *, t):
    qi = pl.program_id(1)
    qt = qt_ref[...]
    m_ref[...] = jnp.full(m_ref.shape, -1e30, F32)
    l_ref[...] = jnp.zeros(l_ref.shape, F32)
    acc_ref[...] = jnp.zeros(acc_ref.shape, F32)

    def scores(kj, s_ref):
        start = pl.multiple_of(kj * t, t)
        s_ref[...] = jnp.dot(k_ref[pl.ds(start, t), :], qt, preferred_element_type=F32)

    def update(kj, s_ref, masked):
        s = s_ref[...]
        if masked:
            kc = lax.broadcasted_iota(jnp.int32, s.shape, 0) // CHUNK
            qc = lax.broadcasted_iota(jnp.int32, s.shape, 1) // CHUNK
            s = jnp.where(kc <= qc, s, -1e30)
        m_old = m_ref[...]
        m_new = jnp.maximum(m_old, jnp.max(s, axis=0, keepdims=True))
        alpha = jnp.exp2(m_old - m_new)
        p = jnp.exp2(s - m_new)
        l_ref[...] = alpha * l_ref[...] + jnp.sum(p, axis=0, keepdims=True)
        acc_ref[...] = alpha * acc_ref[...] + jnp.dot(vt_ref[kj], p.astype(BF16),
                                                      preferred_element_type=F32)
        m_ref[...] = m_new

    scores(0, sa_ref)

    def body(i, carry):
        scores(2 * i + 1, sb_ref)
        update(2 * i, sa_ref, False)
        scores(2 * i + 2, sa_ref)
        update(2 * i + 1, sb_ref, False)
        return carry

    lax.fori_loop(0, qi // 2, body, 0)

    @pl.when(qi % 2 == 0)
    def _():
        update(qi, sa_ref, True)

    @pl.when(qi % 2 == 1)
    def _():
        scores(qi, sb_ref)
        update(qi - 1, sa_ref, False)
        update(qi, sb_ref, True)

    o_ref[...] = (acc_ref[...] / l_ref[...]).T.astype(o_ref.dtype)


def _attention(qt, k, vt, t):
    s = k.shape[0]
    return pl.pallas_call(
        functools.partial(_attn_kernel, t=t),
        grid=(MLA_HEADS, s // t),
        in_specs=[
            pl.BlockSpec((None, HEAD_PAD, t), lambda h, i: (h, 0, i)),
            pl.BlockSpec((s, HEAD_PAD), lambda h, i: (0, h)),
            pl.BlockSpec((None, s // t, MLA_V_DIM, t), lambda h, i: (h, 0, 0, 0)),
        ],
        out_specs=pl.BlockSpec((t, MLA_V_DIM), lambda h, i: (i, h)),
        out_shape=jax.ShapeDtypeStruct((s, MLA_WIDTH), BF16),
        scratch_shapes=[
            pltpu.VMEM((t, t), F32),
            pltpu.VMEM((t, t), F32),
            pltpu.VMEM((1, t), F32),
            pltpu.VMEM((1, t), F32),
            pltpu.VMEM((MLA_V_DIM, t), F32),
        ],
        compiler_params=_cparams(("arbitrary", "arbitrary")),
        name="mla_attention",
    )(qt, k, vt)


def _gla_kernel(q_ref, k_ref, v_ref, g_ref, kra_ref, wg_ref, bg_ref, on_ref, tri_ref, o_ref,
                st_ref, *, tm):
    @pl.when(pl.program_id(0) == 0)
    def _():
        st_ref[...] = jnp.zeros(st_ref.shape, F32)

    pre = jnp.dot(kra_ref[...], wg_ref[...], preferred_element_type=F32) + bg_ref[...]
    log_a = (jnp.minimum(pre, 0.0) - jnp.log1p(jnp.exp(-jnp.abs(pre)))) * (1.0 / GATE_TAU)
    hi = log_a.astype(BF16)
    lo = (log_a - hi.astype(F32)).astype(BF16)
    tri = tri_ref[...]
    suffix = (jnp.dot(tri, hi, preferred_element_type=F32)
              + jnp.dot(tri, lo, preferred_element_type=F32))
    kd = k_ref[...].astype(F32) * jnp.exp(suffix)
    row = lax.broadcasted_iota(jnp.int32, kd.shape, 0)
    first_half = ((row // CHUNK) % 2) == 0
    kd_pair = (jnp.where(first_half, kd, 0.0).astype(BF16),
               jnp.where(first_half, 0.0, kd).astype(BF16))
    vt = v_ref[...].astype(F32).T.astype(BF16)
    on = on_ref[...]
    scale = GLA_DK ** -0.5

    for c in range(tm // CHUNK):
        r0 = c * CHUNK
        p0 = (c // 2) * 2 * CHUNK
        decay = jnp.exp(jnp.sum(log_a[r0:r0 + CHUNK, :], axis=0, keepdims=True))
        for h in range(GLA_HEADS):
            kcols = slice(h * GLA_DK, (h + 1) * GLA_DK)
            vcols = slice(h * GLA_DV, (h + 1) * GLA_DV)
            ut = jnp.dot(vt[vcols, p0:p0 + 2 * CHUNK], kd_pair[c % 2][p0:p0 + 2 * CHUNK, kcols],
                         preferred_element_type=F32)
            state = st_ref[h] * decay[:, kcols] + ut
            st_ref[h] = state
            o = lax.dot_general(q_ref[r0:r0 + CHUNK, kcols], state.astype(BF16),
                                (((1,), (1,)), ((), ())), preferred_element_type=F32) * scale
            o = _rms(o, on)
            gate = g_ref[r0:r0 + CHUNK, vcols].astype(F32)
            o_ref[r0:r0 + CHUNK, vcols] = (o * (gate * jax.nn.sigmoid(gate))).astype(o_ref.dtype)


def _gla(z, wg, bg, on, tri, tm):
    s = z.shape[0]
    full = lambda a: pl.BlockSpec(a.shape, lambda i: (0, 0))
    return pl.pallas_call(
        functools.partial(_gla_kernel, tm=tm),
        grid=(s // tm,),
        in_specs=[
            pl.BlockSpec((tm, GLA_K_WIDTH), lambda i: (i, Z_Q // GLA_K_WIDTH)),
            pl.BlockSpec((tm, GLA_K_WIDTH), lambda i: (i, Z_K // GLA_K_WIDTH)),
            pl.BlockSpec((tm, GLA_WIDTH), lambda i: (i, Z_V // GLA_WIDTH)),
            pl.BlockSpec((tm, GLA_WIDTH), lambda i: (i, Z_G // GLA_WIDTH)),
            pl.BlockSpec((tm, LANES), lambda i: (i, Z_KRA // LANES)),
            full(wg), full(bg), full(on), full(tri),
        ],
        out_specs=pl.BlockSpec((tm, GLA_WIDTH), lambda i: (i, 0)),
        out_shape=jax.ShapeDtypeStruct((s, GLA_WIDTH), BF16),
        scratch_shapes=[pltpu.VMEM((GLA_HEADS, GLA_DV, GLA_DK), F32)],
        compiler_params=_cparams(("arbitrary",)),
        name="gla",
    )(z, z, z, z, z, wg, bg, on, tri)


def _out_proj_kernel(a_ref, g_ref, x_ref, an_ref, w_ref, o_ref):
    an = _rms(a_ref[...].astype(F32), an_ref[...]).astype(BF16)
    acc = jnp.dot(an, w_ref[:MLA_WIDTH, :], preferred_element_type=F32)
    acc = acc + jnp.dot(g_ref[...], w_ref[MLA_WIDTH:, :], preferred_element_type=F32)
    o_ref[...] = x_ref[...] + acc


def _out_proj(a, g, x, an, w, tm):
    s, d = x.shape
    return pl.pallas_call(
        _out_proj_kernel,
        grid=(s // tm,),
        in_specs=[
            pl.BlockSpec((tm, MLA_WIDTH), lambda i: (i, 0)),
            pl.BlockSpec((tm, GLA_WIDTH), lambda i: (i, 0)),
            pl.BlockSpec((tm, d), lambda i: (i, 0)),
            pl.BlockSpec((1, MLA_WIDTH), lambda i: (0, 0)),
            pl.BlockSpec((MIX_WIDTH, d), lambda i: (0, 0)),
        ],
        out_specs=pl.BlockSpec((tm, d), lambda i: (i, 0)),
        out_shape=jax.ShapeDtypeStruct((s, d), F32),
        compiler_params=_cparams(("arbitrary",)),
        name="out_proj",
    )(a, g, x, an, w)


HALO = 8
FIX = 16


def _ffn_kernel(x_ref, g_ref, wg_ref, wu_ref, cwg_ref, cwu_ref, cbg_ref, cbu_ref, wd_ref, o_ref,
                hn_ref, carry_ref, act_ref, *, tm):
    i = pl.program_id(0)
    f = pl.program_id(1)

    @pl.when(f == 0)
    def _():
        x = x_ref[...]
        hn_ref[...] = _rms(x, g_ref[...]).astype(BF16)
        o_ref[...] = x

    @pl.when(i == 0)
    def _():
        carry_ref[f] = jnp.zeros(carry_ref.shape[1:], F32)

    hn = hn_ref[...]

    def conv(u, cw_ref, cb_ref, slot):
        w0, w1, w2 = cw_ref[0:1, :], cw_ref[1:2, :], cw_ref[2:3, :]
        cb = cb_ref[...]
        y = cb + w0 * pltpu.roll(u, 2, 0) + w1 * pltpu.roll(u, 1, 0) + w2 * u
        ext = jnp.concatenate([carry_ref[f, slot], u[:FIX, :]], axis=0)
        y_fix = (cb + w0 * pltpu.roll(ext, 2, 0)[HALO:, :] + w1 * pltpu.roll(ext, 1, 0)[HALO:, :]
                 + w2 * ext[HALO:, :])
        carry_ref[f, slot] = u[tm - HALO:, :]
        return y, y_fix

    ug = jnp.dot(hn, wg_ref[...], preferred_element_type=F32)
    yg, yg_fix = conv(ug, cwg_ref, cbg_ref, 0)
    uu = jnp.dot(hn, wu_ref[...], preferred_element_type=F32)
    yu, yu_fix = conv(uu, cwu_ref, cbu_ref, 1)
    act_ref[...] = (yg * jax.nn.sigmoid(yg) * yu).astype(BF16)
    act_ref[0:FIX, :] = (yg_fix * jax.nn.sigmoid(yg_fix) * yu_fix).astype(BF16)
    o_ref[...] += jnp.dot(act_ref[...], wd_ref[...], preferred_element_type=F32)


def _ffn(x, gain, w_up, conv_w, conv_b, w_down, tm, tf):
    s, d = x.shape
    nf = D_FF // tf
    return pl.pallas_call(
        functools.partial(_ffn_kernel, tm=tm),
        grid=(s // tm, nf),
        in_specs=[
            pl.BlockSpec((tm, d), lambda i, f: (i, 0)),
            pl.BlockSpec((1, d), lambda i, f: (0, 0)),
            pl.BlockSpec((d, tf), lambda i, f: (0, f)),
            pl.BlockSpec((d, tf), lambda i, f: (0, f + nf)),
            pl.BlockSpec((CONV_W, tf), lambda i, f: (0, f)),
            pl.BlockSpec((CONV_W, tf), lambda i, f: (0, f + nf)),
            pl.BlockSpec((1, tf), lambda i, f: (0, f)),
            pl.BlockSpec((1, tf), lambda i, f: (0, f + nf)),
            pl.BlockSpec((tf, d), lambda i, f: (f, 0)),
        ],
        out_specs=pl.BlockSpec((tm, d), lambda i, f: (i, 0)),
        out_shape=jax.ShapeDtypeStruct((s, d), F32),
        scratch_shapes=[
            pltpu.VMEM((tm, d), BF16),
            pltpu.VMEM((nf, 2, HALO, tf), F32),
            pltpu.VMEM((tm, tf), BF16),
        ],
        compiler_params=_cparams(("arbitrary", "arbitrary")),
        name="conv_ffn",
    )(x, gain, w_up, w_up, conv_w, conv_w, conv_b, conv_b, w_down)


def _pack_w_in(w):
    d = w.shape[0]
    o_cq, o_ckv, o_kr = 0, Q_LORA, Q_LORA + KV_LORA
    o_q = o_kr + ROPE_DIM
    o_k = o_q + GLA_K_WIDTH
    o_v = o_k + GLA_K_WIDTH
    o_a = o_v + GLA_WIDTH
    o_g = o_a + GATE_RANK
    zeros = lambda n: jnp.zeros((d, n), w.dtype)
    kra = jnp.concatenate([
        w[:, o_kr:o_kr + ROPE_HALF], w[:, o_a:o_a + GATE_RANK], zeros(KRA_X2_OFF - KRA_A_OFF - GATE_RANK),
        w[:, o_kr + ROPE_HALF:o_kr + ROPE_DIM], zeros(LANES - KRA_X2_OFF - ROPE_HALF)], axis=1)
    packed = jnp.concatenate([
        w[:, o_cq:o_cq + Q_LORA], w[:, o_q:o_q + GLA_K_WIDTH], w[:, o_k:o_k + GLA_K_WIDTH],
        w[:, o_ckv:o_ckv + KV_LORA], kra, zeros(Z_V - Z_KRA - LANES),
        w[:, o_v:o_v + GLA_WIDTH], w[:, o_g:o_g + GLA_WIDTH]], axis=1)
    return packed.astype(BF16)


def _pack_head_vec(g):
    z = jnp.zeros((ROPE_HALF,), g.dtype)
    return jnp.concatenate([g[:NOPE_DIM], g[NOPE_DIM:NOPE_DIM + ROPE_HALF], z,
                            g[NOPE_DIM + ROPE_HALF:], z])[None, :]


def _pack_w_uq(w):
    r = w.shape[0]
    w = w.reshape(r, MLA_HEADS, QK_HEAD)
    z = jnp.zeros((r, MLA_HEADS, ROPE_HALF), w.dtype)
    w = jnp.concatenate([w[..., :NOPE_DIM], w[..., NOPE_DIM:NOPE_DIM + ROPE_HALF], z,
                         w[..., NOPE_DIM + ROPE_HALF:], z], axis=-1)
    return w.reshape(r, MLA_HEADS * HEAD_PAD).astype(BF16)


def _pack_w_ukv(w):
    r = w.shape[0]
    w = w.reshape(r, MLA_HEADS, NOPE_DIM + MLA_V_DIM)
    return jnp.concatenate([w[..., :NOPE_DIM].reshape(r, -1), w[..., NOPE_DIM:].reshape(r, -1)],
                           axis=1).astype(BF16)


def _pack_w_gate(w):
    z = lambda n: jnp.zeros((n, w.shape[1]), w.dtype)
    return jnp.concatenate([z(KRA_A_OFF), w, z(LANES - KRA_A_OFF - GATE_RANK)], axis=0).astype(BF16)


def _rope_tables(positions):
    inv_freq = ROPE_BASE ** (-jnp.arange(0, ROPE_DIM, 2, dtype=F32) / ROPE_DIM)
    angle = positions.astype(F32)[:, None] * inv_freq
    cos, sin = jnp.cos(angle), jnp.sin(angle)
    z = jnp.zeros_like(cos)
    return (jnp.concatenate([cos, z, cos, z], axis=1), jnp.concatenate([-sin, z, sin, z], axis=1))


def _suffix_matrix(tm):
    t = jnp.arange(tm)
    same = (t[:, None] // CHUNK) == (t[None, :] // CHUNK)
    return (same & (t[None, :] > t[:, None])).astype(BF16)


def kernel(x, positions, attn_norm, w_in, cq_norm, w_uq, ckv_norm, w_ukv, q_norm, k_norm, w_gate,
           b_gate, gla_out_norm, mla_out_norm, w_out, ffn_norm, w_up, conv_w, conv_b, w_down):
    batch, seq, d = x.shape
    assert batch == 1 and d == D_MODEL
    tm_proj = min(1024, seq)
    tm = min(512, seq)
    tm_gla = min(256, seq)
    ct, st = _rope_tables(positions[0])
    tri = _suffix_matrix(tm_gla)
    q_scale = math.log2(math.e) / math.sqrt(QK_HEAD)
    h = x[0]
    for l in range(DEPTH):
        z = _in_proj(h, attn_norm[l][None, :], _pack_w_in(w_in[l]), tm_proj, 1024)
        q, k, v = _mla_prep(z, ct, st, cq_norm[l][None, :], ckv_norm[l][None, :],
                            _pack_w_uq(w_uq[l]), _pack_w_ukv(w_ukv[l]),
                            _pack_head_vec(q_norm[l]) * q_scale, _pack_head_vec(k_norm[l]), tm)
        a = _attention(q, k, v, tm)
        g = _gla(z, _pack_w_gate(w_gate[l]), b_gate[l][None, :], gla_out_norm[l][None, :], tri,
                 tm_gla)
        h = _out_proj(a, g, h, mla_out_norm[l][None, :], w_out[l].astype(BF16), tm)
        h = _ffn(h, ffn_norm[l][None, :], w_up[l].astype(BF16), conv_w[l], conv_b[l][None, :],
                 w_down[l].astype(BF16), tm, 512)
    return h[None]
```

```python
import functools
import math

import jax
import jax.numpy as jnp
from jax import lax
from jax.experimental import pallas as pl
from jax.experimental.pallas import tpu as pltpu

F32 = jnp.float32
BF16 = jnp.bfloat16

D_MODEL = 2048
DEPTH = 2
CHUNK = 64
MLA_HEADS = 8
Q_LORA = 512
KV_LORA = 256
NOPE_DIM = 128
ROPE_DIM = 64
ROPE_HALF = ROPE_DIM // 2
QK_HEAD = NOPE_DIM + ROPE_DIM
MLA_V_DIM = 128
MLA_WIDTH = MLA_HEADS * MLA_V_DIM
ROPE_BASE = 10000.0
GLA_HEADS = 4
GLA_DK = 128
GLA_DV = 256
GLA_K_WIDTH = GLA_HEADS * GLA_DK
GLA_WIDTH = GLA_HEADS * GLA_DV
GATE_RANK = 16
GATE_TAU = 16.0
MIX_WIDTH = MLA_WIDTH + GLA_WIDTH
D_FF = 5632
CONV_W = 3
NORM_EPS = 1e-6

LANES = 128
HEAD_PAD = 2 * LANES

Z_CQ, Z_Q, Z_K, Z_CKV, Z_KRA, Z_V, Z_G = 0, 512, 1024, 1536, 1792, 2048, 3072
Z_WIDTH = 4096
KRA_A_OFF = ROPE_HALF
KRA_X2_OFF = 2 * ROPE_HALF

VMEM_LIMIT = 56 * 1024 * 1024


def _cparams(sem):
    return pltpu.CompilerParams(dimension_semantics=sem, vmem_limit_bytes=VMEM_LIMIT)


def _rms(x, gain):
    ms = jnp.mean(x * x, axis=-1, keepdims=True)
    return x * lax.rsqrt(ms + NORM_EPS) * gain


def _in_proj_kernel(x_ref, g_ref, w_ref, o_ref, hn_ref):
    @pl.when(pl.program_id(1) == 0)
    def _():
        hn_ref[...] = _rms(x_ref[...], g_ref[...]).astype(BF16)

    o_ref[...] = jnp.dot(hn_ref[...], w_ref[...], preferred_element_type=F32).astype(o_ref.dtype)


def _in_proj(x, gain, w, tm, tn):
    s, d = x.shape
    n = w.shape[1]
    return pl.pallas_call(
        _in_proj_kernel,
        grid=(s // tm, n // tn),
        in_specs=[
            pl.BlockSpec((tm, d), lambda i, j: (i, 0)),
            pl.BlockSpec((1, d), lambda i, j: (0, 0)),
            pl.BlockSpec((d, tn), lambda i, j: (0, j)),
        ],
        out_specs=pl.BlockSpec((tm, tn), lambda i, j: (i, j)),
        out_shape=jax.ShapeDtypeStruct((s, n), BF16),
        scratch_shapes=[pltpu.VMEM((tm, d), BF16)],
        compiler_params=_cparams(("arbitrary", "arbitrary")),
        name="in_proj",
    )(x, gain, w)


def _rope(b, ct, st):
    return b * ct + pltpu.roll(b, KRA_X2_OFF, 1) * st


def _mla_prep_kernel(cq_ref, ckv_ref, kra_ref, ct_ref, st_ref, cqg_ref, ckvg_ref, wuq_ref,
                     wukv_ref, qg_ref, kg_ref, qt_ref, k_ref, vt_ref):
    ct = ct_ref[...]
    st = st_ref[...]
    inv_head = 1.0 / QK_HEAD

    cqn = _rms(cq_ref[...].astype(F32), cqg_ref[...]).astype(BF16)
    qall = jnp.dot(cqn, wuq_ref[...], preferred_element_type=F32)
    qg = qg_ref[...]
    for h in range(MLA_HEADS):
        qh = qall[:, h * HEAD_PAD:(h + 1) * HEAD_PAD]
        ssq = jnp.sum(qh * qh, axis=-1, keepdims=True)
        qh = qh * lax.rsqrt(ssq * inv_head + NORM_EPS) * qg
        qt_ref[h, :LANES, :] = qh[:, :LANES].T.astype(BF16)
        qt_ref[h, LANES:, :] = _rope(qh[:, LANES:], ct, st).T.astype(BF16)

    ckvn = _rms(ckv_ref[...].astype(F32), ckvg_ref[...]).astype(BF16)
    kv = jnp.dot(ckvn, wukv_ref[...], preferred_element_type=F32)
    kg = kg_ref[...]
    kg_n = kg[:, :LANES]
    kg_r = kg[:, LANES:]
    kr = kra_ref[...].astype(F32)
    lane = lax.broadcasted_iota(jnp.int32, kr.shape, 1)
    is_rope = (lane < ROPE_HALF) | ((lane >= KRA_X2_OFF) & (lane < KRA_X2_OFF + ROPE_HALF))
    kr = jnp.where(is_rope, kr, 0.0)
    ssq_r = jnp.sum(kr * kr, axis=-1, keepdims=True)
    kr_rot = _rope(kr * kg_r, ct, st)
    for h in range(MLA_HEADS):
        kn = kv[:, h * NOPE_DIM:(h + 1) * NOPE_DIM]
        ssq = jnp.sum(kn * kn, axis=-1, keepdims=True) + ssq_r
        r = lax.rsqrt(ssq * inv_head + NORM_EPS)
        k_ref[:, h * HEAD_PAD:h * HEAD_PAD + LANES] = (kn * r * kg_n).astype(BF16)
        k_ref[:, h * HEAD_PAD + LANES:(h + 1) * HEAD_PAD] = (kr_rot * r).astype(BF16)
        v0 = MLA_HEADS * NOPE_DIM + h * MLA_V_DIM
        vt_ref[h, 0] = kv[:, v0:v0 + MLA_V_DIM].T.astype(BF16)


def _mla_prep(z, ct, st, cqg, ckvg, wuq, wukv, qg, kg, tm):
    s = z.shape[0]
    full = lambda a: pl.BlockSpec(a.shape, lambda i: (0, 0))
    return pl.pallas_call(
        _mla_prep_kernel,
        grid=(s // tm,),
        in_specs=[
            pl.BlockSpec((tm, Q_LORA), lambda i: (i, Z_CQ // Q_LORA)),
            pl.BlockSpec((tm, KV_LORA), lambda i: (i, Z_CKV // KV_LORA)),
            pl.BlockSpec((tm, LANES), lambda i: (i, Z_KRA // LANES)),
            pl.BlockSpec((tm, LANES), lambda i: (i, 0)),
            pl.BlockSpec((tm, LANES), lambda i: (i, 0)),
            full(cqg), full(ckvg), full(wuq), full(wukv), full(qg), full(kg),
        ],
        out_specs=[
            pl.BlockSpec((MLA_HEADS, HEAD_PAD, tm), lambda i: (0, 0, i)),
            pl.BlockSpec((tm, MLA_HEADS * HEAD_PAD), lambda i: (i, 0)),
            pl.BlockSpec((MLA_HEADS, 1, MLA_V_DIM, tm), lambda i: (0, i, 0, 0)),
        ],
        out_shape=[
            jax.ShapeDtypeStruct((MLA_HEADS, HEAD_PAD, s), BF16),
            jax.ShapeDtypeStruct((s, MLA_HEADS * HEAD_PAD), BF16),
            jax.ShapeDtypeStruct((MLA_HEADS, s // tm, MLA_V_DIM, tm), BF16),
        ],
        compiler_params=_cparams(("arbitrary",)),
        name="mla_prep",
    )(z, z, z, ct, st, cqg, ckvg, wuq, wukv, qg, kg)


def _attn_kernel(qt_ref, k_ref, vt_ref, o_ref, sa_ref, sb_ref, m_ref, l_ref, acc_ref, *, tk):
    qi = pl.program_id(1)
    m_ref[...] = jnp.full(m_ref.shape, -1e30, F32)
    l_ref[...] = jnp.zeros(l_ref.shape, F32)
    acc_ref[...] = jnp.zeros(acc_ref.shape, F32)

    def scores(kj, s_ref, lo):
        start = pl.multiple_of(kj * tk, tk)
        s_ref[:, lo:] = jnp.dot(k_ref[pl.ds(start, tk), :], qt_ref[:, lo:],
                                preferred_element_type=F32)

    def update(kj, s_ref, lo, masked):
        s = s_ref[:, lo:]
        if masked:
            kc = lax.broadcasted_iota(jnp.int32, s.shape, 0) // CHUNK
            qc = lax.broadcasted_iota(jnp.int32, s.shape, 1) // CHUNK
            s = jnp.where(kc <= qc, s, -1e30)
        m_old = m_ref[:, lo:]
        m_new = jnp.maximum(m_old, jnp.max(s, axis=0, keepdims=True))
        alpha = jnp.exp2(m_old - m_new)
        p = jnp.exp2(s - m_new)
        l_ref[:, lo:] = alpha * l_ref[:, lo:] + jnp.sum(p, axis=0, keepdims=True)
        acc_ref[:, lo:] = alpha * acc_ref[:, lo:] + jnp.dot(vt_ref[kj], p.astype(BF16),
                                                            preferred_element_type=F32)
        m_ref[:, lo:] = m_new

    scores(0, sa_ref, 0)

    def body(i, carry):
        scores(2 * i + 1, sb_ref, 0)
        update(2 * i, sa_ref, 0, False)
        scores(2 * i + 2, sa_ref, 0)
        update(2 * i + 1, sb_ref, 0, False)
        return carry

    lax.fori_loop(0, qi, body, 0)
    scores(2 * qi + 1, sb_ref, tk)
    update(2 * qi, sa_ref, 0, True)
    update(2 * qi + 1, sb_ref, tk, True)
    o_ref[...] = (acc_ref[...] / l_ref[...]).T.astype(o_ref.dtype)


def _attention(qt, k, vt, tk):
    s = k.shape[0]
    tq = 2 * tk
    return pl.pallas_call(
        functools.partial(_attn_kernel, tk=tk),
        grid=(MLA_HEADS, s // tq),
        in_specs=[
            pl.BlockSpec((None, HEAD_PAD, tq), lambda h, i: (h, 0, i)),
            pl.BlockSpec((s, HEAD_PAD), lambda h, i: (0, h)),
            pl.BlockSpec((None, s // tk, MLA_V_DIM, tk), lambda h, i: (h, 0, 0, 0)),
        ],
        out_specs=pl.BlockSpec((tq, MLA_V_DIM), lambda h, i: (i, h)),
        out_shape=jax.ShapeDtypeStruct((s, MLA_WIDTH), BF16),
        scratch_shapes=[
            pltpu.VMEM((tk, tq), F32),
            pltpu.VMEM((tk, tq), F32),
            pltpu.VMEM((1, tq), F32),
            pltpu.VMEM((1, tq), F32),
            pltpu.VMEM((MLA_V_DIM, tq), F32),
        ],
        compiler_params=_cparams(("arbitrary", "arbitrary")),
        name="mla_attention",
    )(qt, k, vt)


def _gla_kernel(q_ref, k_ref, v_ref, g_ref, kra_ref, wg_ref, bg_ref, on_ref, tri_ref, o_ref,
                st_ref, *, tm):
    @pl.when(pl.program_id(0) == 0)
    def _():
        st_ref[...] = jnp.zeros(st_ref.shape, F32)

    pre = jnp.dot(kra_ref[...], wg_ref[...], preferred_element_type=F32) + bg_ref[...]
    log_a = (jnp.minimum(pre, 0.0) - jnp.log1p(jnp.exp(-jnp.abs(pre)))) * (1.0 / GATE_TAU)
    hi = log_a.astype(BF16)
    lo = (log_a - hi.astype(F32)).astype(BF16)
    tri = tri_ref[...]
    suffix = (jnp.dot(tri, hi, preferred_element_type=F32)
              + jnp.dot(tri, lo, preferred_element_type=F32))
    kd = k_ref[...].astype(F32) * jnp.exp(suffix)
    row = lax.broadcasted_iota(jnp.int32, kd.shape, 0)
    first_half = ((row // CHUNK) % 2) == 0
    kd_pair = (jnp.where(first_half, kd, 0.0).astype(BF16),
               jnp.where(first_half, 0.0, kd).astype(BF16))
    vt = v_ref[...].astype(F32).T.astype(BF16)
    on = on_ref[...]
    scale = GLA_DK ** -0.5

    n_chunks = tm // CHUNK
    kcols = [slice(h * GLA_DK, (h + 1) * GLA_DK) for h in range(GLA_HEADS)]
    vcols = [slice(h * GLA_DV, (h + 1) * GLA_DV) for h in range(GLA_HEADS)]

    uts = []
    for c in range(n_chunks):
        p0 = (c // 2) * 2 * CHUNK
        uts.append([jnp.dot(vt[vcols[h], p0:p0 + 2 * CHUNK],
                            kd_pair[c % 2][p0:p0 + 2 * CHUNK, kcols[h]],
                            preferred_element_type=F32) for h in range(GLA_HEADS)])

    cur = [st_ref[h] for h in range(GLA_HEADS)]
    states = []
    for c in range(n_chunks):
        r0 = c * CHUNK
        decay = jnp.exp(jnp.sum(log_a[r0:r0 + CHUNK, :], axis=0, keepdims=True))
        cur = [cur[h] * decay[:, kcols[h]] + uts[c][h] for h in range(GLA_HEADS)]
        states.append([s.astype(BF16) for s in cur])
    for h in range(GLA_HEADS):
        st_ref[h] = cur[h]

    for c in range(n_chunks):
        r0 = c * CHUNK
        for h in range(GLA_HEADS):
            o = lax.dot_general(q_ref[r0:r0 + CHUNK, kcols[h]], states[c][h],
                                (((1,), (1,)), ((), ())), preferred_element_type=F32) * scale
            o = _rms(o, on)
            gate = g_ref[r0:r0 + CHUNK, vcols[h]].astype(F32)
            o_ref[r0:r0 + CHUNK, vcols[h]] = (o * (gate * jax.nn.sigmoid(gate))).astype(o_ref.dtype)


def _gla(z, wg, bg, on, tri, tm):
    s = z.shape[0]
    full = lambda a: pl.BlockSpec(a.shape, lambda i: (0, 0))
    return pl.pallas_call(
        functools.partial(_gla_kernel, tm=tm),
        grid=(s // tm,),
        in_specs=[
            pl.BlockSpec((tm, GLA_K_WIDTH), lambda i: (i, Z_Q // GLA_K_WIDTH)),
            pl.BlockSpec((tm, GLA_K_WIDTH), lambda i: (i, Z_K // GLA_K_WIDTH)),
            pl.BlockSpec((tm, GLA_WIDTH), lambda i: (i, Z_V // GLA_WIDTH)),
            pl.BlockSpec((tm, GLA_WIDTH), lambda i: (i, Z_G // GLA_WIDTH)),
            pl.BlockSpec((tm, LANES), lambda i: (i, Z_KRA // LANES)),
            full(wg), full(bg), full(on), full(tri),
        ],
        out_specs=pl.BlockSpec((tm, GLA_WIDTH), lambda i: (i, 0)),
        out_shape=jax.ShapeDtypeStruct((s, GLA_WIDTH), BF16),
        scratch_shapes=[pltpu.VMEM((GLA_HEADS, GLA_DV, GLA_DK), F32)],
        compiler_params=_cparams(("arbitrary",)),
        name="gla",
    )(z, z, z, z, z, wg, bg, on, tri)


def _out_proj_kernel(a_ref, g_ref, x_ref, an_ref, w_ref, o_ref):
    an = _rms(a_ref[...].astype(F32), an_ref[...]).astype(BF16)
    acc = jnp.dot(an, w_ref[:MLA_WIDTH, :], preferred_element_type=F32)
    acc = acc + jnp.dot(g_ref[...], w_ref[MLA_WIDTH:, :], preferred_element_type=F32)
    o_ref[...] = x_ref[...] + acc


def _out_proj(a, g, x, an, w, tm):
    s, d = x.shape
    return pl.pallas_call(
        _out_proj_kernel,
        grid=(s // tm,),
        in_specs=[
            pl.BlockSpec((tm, MLA_WIDTH), lambda i: (i, 0)),
            pl.BlockSpec((tm, GLA_WIDTH), lambda i: (i, 0)),
            pl.BlockSpec((tm, d), lambda i: (i, 0)),
            pl.BlockSpec((1, MLA_WIDTH), lambda i: (0, 0)),
            pl.BlockSpec((MIX_WIDTH, d), lambda i: (0, 0)),
        ],
        out_specs=pl.BlockSpec((tm, d), lambda i: (i, 0)),
        out_shape=jax.ShapeDtypeStruct((s, d), F32),
        compiler_params=_cparams(("arbitrary",)),
        name="out_proj",
    )(a, g, x, an, w)


HALO = 8
FIX = 16


def _ffn_kernel(x_ref, g_ref, wg_ref, wu_ref, cwg_ref, cwu_ref, cbg_ref, cbu_ref, wd_ref, o_ref,
                hn_ref, carry_ref, act_ref, *, tm):
    i = pl.program_id(0)
    f = pl.program_id(1)

    @pl.when(f == 0)
    def _():
        x = x_ref[...]
        hn_ref[...] = _rms(x, g_ref[...]).astype(BF16)
        o_ref[...] = x

    @pl.when(i == 0)
    def _():
        carry_ref[f] = jnp.zeros(carry_ref.shape[1:], F32)

    hn = hn_ref[...]

    def conv(u, cw_ref, cb_ref, slot):
        w0, w1, w2 = cw_ref[0:1, :], cw_ref[1:2, :], cw_ref[2:3, :]
        cb = cb_ref[...]
        y = cb + w0 * pltpu.roll(u, 2, 0) + w1 * pltpu.roll(u, 1, 0) + w2 * u
        ext = jnp.concatenate([carry_ref[f, slot], u[:FIX, :]], axis=0)
        y_fix = (cb + w0 * pltpu.roll(ext, 2, 0)[HALO:, :] + w1 * pltpu.roll(ext, 1, 0)[HALO:, :]
                 + w2 * ext[HALO:, :])
        carry_ref[f, slot] = u[tm - HALO:, :]
        return y, y_fix

    ug = jnp.dot(hn, wg_ref[...], preferred_element_type=F32)
    yg, yg_fix = conv(ug, cwg_ref, cbg_ref, 0)
    uu = jnp.dot(hn, wu_ref[...], preferred_element_type=F32)
    yu, yu_fix = conv(uu, cwu_ref, cbu_ref, 1)
    act_ref[...] = (yg * jax.nn.sigmoid(yg) * yu).astype(BF16)
    act_ref[0:FIX, :] = (yg_fix * jax.nn.sigmoid(yg_fix) * yu_fix).astype(BF16)
    o_ref[...] += jnp.dot(act_ref[...], wd_ref[...], preferred_element_type=F32)


def _ffn(x, gain, w_up, conv_w, conv_b, w_down, tm, tf):
    s, d = x.shape
    nf = D_FF // tf
    return pl.pallas_call(
        functools.partial(_ffn_kernel, tm=tm),
        grid=(s // tm, nf),
        in_specs=[
            pl.BlockSpec((tm, d), lambda i, f: (i, 0), pipeline_mode=pl.Buffered(1)),
            pl.BlockSpec((1, d), lambda i, f: (0, 0)),
            pl.BlockSpec((d, tf), lambda i, f: (0, f)),
            pl.BlockSpec((d, tf), lambda i, f: (0, f + nf)),
            pl.BlockSpec((CONV_W, tf), lambda i, f: (0, f)),
            pl.BlockSpec((CONV_W, tf), lambda i, f: (0, f + nf)),
            pl.BlockSpec((1, tf), lambda i, f: (0, f)),
            pl.BlockSpec((1, tf), lambda i, f: (0, f + nf)),
            pl.BlockSpec((tf, d), lambda i, f: (f, 0)),
        ],
        out_specs=pl.BlockSpec((tm, d), lambda i, f: (i, 0)),
        out_shape=jax.ShapeDtypeStruct((s, d), F32),
        scratch_shapes=[
            pltpu.VMEM((tm, d), BF16),
            pltpu.VMEM((nf, 2, HALO, tf), F32),
            pltpu.VMEM((tm, tf), BF16),
        ],
        compiler_params=_cparams(("arbitrary", "arbitrary")),
        name="conv_ffn",
    )(x, gain, w_up, w_up, conv_w, conv_w, conv_b, conv_b, w_down)


def _pack_w_in(w):
    d = w.shape[0]
    o_cq, o_ckv, o_kr = 0, Q_LORA, Q_LORA + KV_LORA
    o_q = o_kr + ROPE_DIM
    o_k = o_q + GLA_K_WIDTH
    o_v = o_k + GLA_K_WIDTH
    o_a = o_v + GLA_WIDTH
    o_g = o_a + GATE_RANK
    zeros = lambda n: jnp.zeros((d, n), w.dtype)
    kra = jnp.concatenate([
        w[:, o_kr:o_kr + ROPE_HALF], w[:, o_a:o_a + GATE_RANK], zeros(KRA_X2_OFF - KRA_A_OFF - GATE_RANK),
        w[:, o_kr + ROPE_HALF:o_kr + ROPE_DIM], zeros(LANES - KRA_X2_OFF - ROPE_HALF)], axis=1)
    packed = jnp.concatenate([
        w[:, o_cq:o_cq + Q_LORA], w[:, o_q:o_q + GLA_K_WIDTH], w[:, o_k:o_k + GLA_K_WIDTH],
        w[:, o_ckv:o_ckv + KV_LORA], kra, zeros(Z_V - Z_KRA - LANES),
        w[:, o_v:o_v + GLA_WIDTH], w[:, o_g:o_g + GLA_WIDTH]], axis=1)
    return packed.astype(BF16)


def _pack_head_vec(g):
    z = jnp.zeros((ROPE_HALF,), g.dtype)
    return jnp.concatenate([g[:NOPE_DIM], g[NOPE_DIM:NOPE_DIM + ROPE_HALF], z,
                            g[NOPE_DIM + ROPE_HALF:], z])[None, :]


def _pack_w_uq(w):
    r = w.shape[0]
    w = w.reshape(r, MLA_HEADS, QK_HEAD)
    z = jnp.zeros((r, MLA_HEADS, ROPE_HALF), w.dtype)
    w = jnp.concatenate([w[..., :NOPE_DIM], w[..., NOPE_DIM:NOPE_DIM + ROPE_HALF], z,
                         w[..., NOPE_DIM + ROPE_HALF:], z], axis=-1)
    return w.reshape(r, MLA_HEADS * HEAD_PAD).astype(BF16)


def _pack_w_ukv(w):
    r = w.shape[0]
    w = w.reshape(r, MLA_HEADS, NOPE_DIM + MLA_V_DIM)
    return jnp.concatenate([w[..., :NOPE_DIM].reshape(r, -1), w[..., NOPE_DIM:].reshape(r, -1)],
                           axis=1).astype(BF16)


def _pack_w_gate(w):
    z = lambda n: jnp.zeros((n, w.shape[1]), w.dtype)
    return jnp.concatenate([z(KRA_A_OFF), w, z(LANES - KRA_A_OFF - GATE_RANK)], axis=0).astype(BF16)


def _rope_tables(positions):
    inv_freq = ROPE_BASE ** (-jnp.arange(0, ROPE_DIM, 2, dtype=F32) / ROPE_DIM)
    angle = positions.astype(F32)[:, None] * inv_freq
    cos, sin = jnp.cos(angle), jnp.sin(angle)
    z = jnp.zeros_like(cos)
    return (jnp.concatenate([cos, z, cos, z], axis=1), jnp.concatenate([-sin, z, sin, z], axis=1))


def _suffix_matrix(tm):
    t = jnp.arange(tm)
    same = (t[:, None] // CHUNK) == (t[None, :] // CHUNK)
    return (same & (t[None, :] > t[:, None])).astype(BF16)


def kernel(x, positions, attn_norm, w_in, cq_norm, w_uq, ckv_norm, w_ukv, q_norm, k_norm, w_gate,
           b_gate, gla_out_norm, mla_out_norm, w_out, ffn_norm, w_up, conv_w, conv_b, w_down):
    batch, seq, d = x.shape
    assert batch == 1 and d == D_MODEL
    tm_proj = min(1024, seq)
    tm = min(512, seq)
    tm_gla = min(256, seq)
    ct, st = _rope_tables(positions[0])
    tri = _suffix_matrix(tm_gla)
    q_scale = math.log2(math.e) / math.sqrt(QK_HEAD)
    h = x[0]
    for l in range(DEPTH):
        z = _in_proj(h, attn_norm[l][None, :], _pack_w_in(w_in[l]), tm_proj, 1024)
        q, k, v = _mla_prep(z, ct, st, cq_norm[l][None, :], ckv_norm[l][None, :],
                            _pack_w_uq(w_uq[l]), _pack_w_ukv(w_ukv[l]),
                            _pack_head_vec(q_norm[l]) * q_scale, _pack_head_vec(k_norm[l]), tm)
        a = _attention(q, k, v, tm)
        g = _gla(z, _pack_w_gate(w_gate[l]), b_gate[l][None, :], gla_out_norm[l][None, :], tri,
                 tm_gla)
        h = _out_proj(a, g, h, mla_out_norm[l][None, :], w_out[l].astype(BF16), tm)
        h = _ffn(h, ffn_norm[l][None, :], w_up[l].astype(BF16), conv_w[l], conv_b[l][None, :],
                 w_down[l].astype(BF16), tm_proj, 512)
    return h[None]
```

```python
import functools
import math

import jax
import jax.numpy as jnp
from jax import lax
from jax.experimental import pallas as pl
from jax.experimental.pallas import tpu as pltpu

F32 = jnp.float32
BF16 = jnp.bfloat16

D_MODEL = 2048
DEPTH = 2
CHUNK = 64
MLA_HEADS = 8
Q_LORA = 512
KV_LORA = 256
NOPE_DIM = 128
ROPE_DIM = 64
ROPE_HALF = ROPE_DIM // 2
QK_HEAD = NOPE_DIM + ROPE_DIM
MLA_V_DIM = 128
MLA_WIDTH = MLA_HEADS * MLA_V_DIM
ROPE_BASE = 10000.0
GLA_HEADS = 4
GLA_DK = 128
GLA_DV = 256
GLA_K_WIDTH = GLA_HEADS * GLA_DK
GLA_WIDTH = GLA_HEADS * GLA_DV
GATE_RANK = 16
GATE_TAU = 16.0
MIX_WIDTH = MLA_WIDTH + GLA_WIDTH
D_FF = 5632
CONV_W = 3
NORM_EPS = 1e-6

LANES = 128
HEAD_PAD = 2 * LANES

Z_CQ, Z_Q, Z_K, Z_CKV, Z_KRA, Z_V, Z_G = 0, 512, 1024, 1536, 1792, 2048, 3072
Z_WIDTH = 4096
KRA_A_OFF = ROPE_HALF
KRA_X2_OFF = 2 * ROPE_HALF

VMEM_LIMIT = 56 * 1024 * 1024

SOFTMAX_SHIFT_LIMIT = 60.0
SCORE_BOUND_MARGIN = 1.02


def _cparams(sem):
    return pltpu.CompilerParams(dimension_semantics=sem, vmem_limit_bytes=VMEM_LIMIT)


def _rms(x, gain):
    ms = jnp.mean(x * x, axis=-1, keepdims=True)
    return x * lax.rsqrt(ms + NORM_EPS) * gain


def _in_proj_kernel(x_ref, g_ref, w_ref, o_ref, hn_ref):
    @pl.when(pl.program_id(1) == 0)
    def _():
        hn_ref[...] = _rms(x_ref[...], g_ref[...]).astype(BF16)

    o_ref[...] = jnp.dot(hn_ref[...], w_ref[...], preferred_element_type=F32).astype(o_ref.dtype)


def _in_proj(x, gain, w, tm, tn):
    s, d = x.shape
    n = w.shape[1]
    return pl.pallas_call(
        _in_proj_kernel,
        grid=(s // tm, n // tn),
        in_specs=[
            pl.BlockSpec((tm, d), lambda i, j: (i, 0)),
            pl.BlockSpec((1, d), lambda i, j: (0, 0)),
            pl.BlockSpec((d, tn), lambda i, j: (0, j)),
        ],
        out_specs=pl.BlockSpec((tm, tn), lambda i, j: (i, j)),
        out_shape=jax.ShapeDtypeStruct((s, n), BF16),
        scratch_shapes=[pltpu.VMEM((tm, d), BF16)],
        compiler_params=_cparams(("arbitrary", "arbitrary")),
        name="in_proj",
    )(x, gain, w)


def _rope(b, ct, st):
    return b * ct + pltpu.roll(b, KRA_X2_OFF, 1) * st


def _mla_prep_kernel(cq_ref, ckv_ref, kra_ref, ct_ref, st_ref, cqg_ref, ckvg_ref, wuq_ref,
                     wukv_ref, qg_ref, kg_ref, qt_ref, k_ref, vt_ref):
    ct = ct_ref[...]
    st = st_ref[...]
    inv_head = 1.0 / QK_HEAD

    cqn = _rms(cq_ref[...].astype(F32), cqg_ref[...]).astype(BF16)
    qall = jnp.dot(cqn, wuq_ref[...], preferred_element_type=F32)
    qg = qg_ref[...]
    for h in range(MLA_HEADS):
        qh = qall[:, h * HEAD_PAD:(h + 1) * HEAD_PAD]
        ssq = jnp.sum(qh * qh, axis=-1, keepdims=True)
        qh = qh * lax.rsqrt(ssq * inv_head + NORM_EPS) * qg
        qt_ref[h, :LANES, :] = qh[:, :LANES].T.astype(BF16)
        qt_ref[h, LANES:, :] = _rope(qh[:, LANES:], ct, st).T.astype(BF16)

    ckvn = _rms(ckv_ref[...].astype(F32), ckvg_ref[...]).astype(BF16)
    kv = jnp.dot(ckvn, wukv_ref[...], preferred_element_type=F32)
    kg = kg_ref[...]
    kg_n = kg[:, :LANES]
    kg_r = kg[:, LANES:]
    kr = kra_ref[...].astype(F32)
    lane = lax.broadcasted_iota(jnp.int32, kr.shape, 1)
    is_rope = (lane < ROPE_HALF) | ((lane >= KRA_X2_OFF) & (lane < KRA_X2_OFF + ROPE_HALF))
    kr = jnp.where(is_rope, kr, 0.0)
    ssq_r = jnp.sum(kr * kr, axis=-1, keepdims=True)
    kr_rot = _rope(kr * kg_r, ct, st)
    for h in range(MLA_HEADS):
        kn = kv[:, h * NOPE_DIM:(h + 1) * NOPE_DIM]
        ssq = jnp.sum(kn * kn, axis=-1, keepdims=True) + ssq_r
        r = lax.rsqrt(ssq * inv_head + NORM_EPS)
        k_ref[:, h * HEAD_PAD:h * HEAD_PAD + LANES] = (kn * r * kg_n).astype(BF16)
        k_ref[:, h * HEAD_PAD + LANES:(h + 1) * HEAD_PAD] = (kr_rot * r).astype(BF16)
        v0 = MLA_HEADS * NOPE_DIM + h * MLA_V_DIM
        vt_ref[h, 0] = kv[:, v0:v0 + MLA_V_DIM].T.astype(BF16)


def _mla_prep(z, ct, st, cqg, ckvg, wuq, wukv, qg, kg, tm):
    s = z.shape[0]
    full = lambda a: pl.BlockSpec(a.shape, lambda i: (0, 0))
    return pl.pallas_call(
        _mla_prep_kernel,
        grid=(s // tm,),
        in_specs=[
            pl.BlockSpec((tm, Q_LORA), lambda i: (i, Z_CQ // Q_LORA)),
            pl.BlockSpec((tm, KV_LORA), lambda i: (i, Z_CKV // KV_LORA)),
            pl.BlockSpec((tm, LANES), lambda i: (i, Z_KRA // LANES)),
            pl.BlockSpec((tm, LANES), lambda i: (i, 0)),
            pl.BlockSpec((tm, LANES), lambda i: (i, 0)),
            full(cqg), full(ckvg), full(wuq), full(wukv), full(qg), full(kg),
        ],
        out_specs=[
            pl.BlockSpec((MLA_HEADS, HEAD_PAD, tm), lambda i: (0, 0, i)),
            pl.BlockSpec((tm, MLA_HEADS * HEAD_PAD), lambda i: (i, 0)),
            pl.BlockSpec((MLA_HEADS, 1, MLA_V_DIM, tm), lambda i: (0, i, 0, 0)),
        ],
        out_shape=[
            jax.ShapeDtypeStruct((MLA_HEADS, HEAD_PAD, s), BF16),
            jax.ShapeDtypeStruct((s, MLA_HEADS * HEAD_PAD), BF16),
            jax.ShapeDtypeStruct((MLA_HEADS, s // tm, MLA_V_DIM, tm), BF16),
        ],
        compiler_params=_cparams(("arbitrary",)),
        name="mla_prep",
    )(z, z, z, ct, st, cqg, ckvg, wuq, wukv, qg, kg)


def _attn_kernel(qt_ref, k_ref, vt_ref, o_ref, sa_ref, sb_ref, m_ref, l_ref, acc_ref, *, tk):
    qi = pl.program_id(1)
    m_ref[...] = jnp.full(m_ref.shape, -1e30, F32)
    l_ref[...] = jnp.zeros(l_ref.shape, F32)
    acc_ref[...] = jnp.zeros(acc_ref.shape, F32)

    def scores(kj, s_ref, lo):
        start = pl.multiple_of(kj * tk, tk)
        s_ref[:, lo:] = jnp.dot(k_ref[pl.ds(start, tk), :], qt_ref[:, lo:],
                                preferred_element_type=F32)

    def update(kj, s_ref, lo, masked):
        s = s_ref[:, lo:]
        if masked:
            kc = lax.broadcasted_iota(jnp.int32, s.shape, 0) // CHUNK
            qc = lax.broadcasted_iota(jnp.int32, s.shape, 1) // CHUNK
            s = jnp.where(kc <= qc, s, -1e30)
        m_old = m_ref[:, lo:]
        m_new = jnp.maximum(m_old, jnp.max(s, axis=0, keepdims=True))
        alpha = jnp.exp2(m_old - m_new)
        p = jnp.exp2(s - m_new)
        l_ref[:, lo:] = alpha * l_ref[:, lo:] + jnp.sum(p, axis=0, keepdims=True)
        acc_ref[:, lo:] = alpha * acc_ref[:, lo:] + jnp.dot(vt_ref[kj], p.astype(BF16),
                                                            preferred_element_type=F32)
        m_ref[:, lo:] = m_new

    scores(0, sa_ref, 0)

    def body(i, carry):
        scores(2 * i + 1, sb_ref, 0)
        update(2 * i, sa_ref, 0, False)
        scores(2 * i + 2, sa_ref, 0)
        update(2 * i + 1, sb_ref, 0, False)
        return carry

    lax.fori_loop(0, qi, body, 0)
    scores(2 * qi + 1, sb_ref, tk)
    update(2 * qi, sa_ref, 0, True)
    update(2 * qi + 1, sb_ref, tk, True)
    o_ref[...] = (acc_ref[...] / l_ref[...]).T.astype(o_ref.dtype)


def _attn_shifted_kernel(qt_ref, k_ref, vt_ref, shift_ref, o_ref, l_ref, acc_ref, *, tk):
    qi = pl.program_id(1)
    l_ref[...] = jnp.zeros(l_ref.shape, F32)
    acc_ref[...] = jnp.zeros(acc_ref.shape, F32)

    def tile(kj, lo, masked):
        start = pl.multiple_of(kj * tk, tk)
        s = jnp.dot(k_ref[pl.ds(start, tk), :], qt_ref[:, lo:], preferred_element_type=F32)
        p = jnp.exp2(s - shift_ref[:, lo:])
        if masked:
            kc = lax.broadcasted_iota(jnp.int32, p.shape, 0) // CHUNK
            qc = lax.broadcasted_iota(jnp.int32, p.shape, 1) // CHUNK
            p = jnp.where(kc <= qc, p, 0.0)
        l_ref[:, lo:] += jnp.sum(p, axis=0, keepdims=True)
        acc_ref[:, lo:] += jnp.dot(vt_ref[kj], p.astype(BF16), preferred_element_type=F32)

    def body(i, carry):
        for u in range(4):
            tile(4 * i + u, 0, False)
        return carry

    lax.fori_loop(0, qi // 2, body, 0)

    @pl.when(qi % 2 == 1)
    def _():
        tile(2 * qi - 2, 0, False)
        tile(2 * qi - 1, 0, False)

    tile(2 * qi, 0, True)
    tile(2 * qi + 1, tk, True)
    o_ref[...] = (acc_ref[...] / l_ref[...]).T.astype(o_ref.dtype)


def _attention_shifted(qt, k, vt, shift, tk):
    s = k.shape[0]
    tq = 2 * tk
    return pl.pallas_call(
        functools.partial(_attn_shifted_kernel, tk=tk),
        grid=(MLA_HEADS, s // tq),
        in_specs=[
            pl.BlockSpec((None, HEAD_PAD, tq), lambda h, i: (h, 0, i)),
            pl.BlockSpec((s, HEAD_PAD), lambda h, i: (0, h)),
            pl.BlockSpec((None, s // tk, MLA_V_DIM, tk), lambda h, i: (h, 0, 0, 0)),
            pl.BlockSpec((1, tq), lambda h, i: (0, 0)),
        ],
        out_specs=pl.BlockSpec((tq, MLA_V_DIM), lambda h, i: (i, h)),
        out_shape=jax.ShapeDtypeStruct((s, MLA_WIDTH), BF16),
        scratch_shapes=[
            pltpu.VMEM((1, tq), F32),
            pltpu.VMEM((MLA_V_DIM, tq), F32),
        ],
        compiler_params=_cparams(("arbitrary", "arbitrary")),
        name="mla_attention_shifted",
    )(qt, k, vt, shift)


def _attention(qt, k, vt, tk):
    s = k.shape[0]
    tq = 2 * tk
    return pl.pallas_call(
        functools.partial(_attn_kernel, tk=tk),
        grid=(MLA_HEADS, s // tq),
        in_specs=[
            pl.BlockSpec((None, HEAD_PAD, tq), lambda h, i: (h, 0, i)),
            pl.BlockSpec((s, HEAD_PAD), lambda h, i: (0, h)),
            pl.BlockSpec((None, s // tk, MLA_V_DIM, tk), lambda h, i: (h, 0, 0, 0)),
        ],
        out_specs=pl.BlockSpec((tq, MLA_V_DIM), lambda h, i: (i, h)),
        out_shape=jax.ShapeDtypeStruct((s, MLA_WIDTH), BF16),
        scratch_shapes=[
            pltpu.VMEM((tk, tq), F32),
            pltpu.VMEM((tk, tq), F32),
            pltpu.VMEM((1, tq), F32),
            pltpu.VMEM((1, tq), F32),
            pltpu.VMEM((MLA_V_DIM, tq), F32),
        ],
        compiler_params=_cparams(("arbitrary", "arbitrary")),
        name="mla_attention",
    )(qt, k, vt)


def _gla_kernel(q_ref, k_ref, v_ref, g_ref, kra_ref, wg_ref, bg_ref, on_ref, tri_ref, o_ref,
                st_ref, *, tm):
    @pl.when(pl.program_id(0) == 0)
    def _():
        st_ref[...] = jnp.zeros(st_ref.shape, F32)

    pre = jnp.dot(kra_ref[...], wg_ref[...], preferred_element_type=F32) + bg_ref[...]
    log_a = (jnp.minimum(pre, 0.0) - jnp.log1p(jnp.exp(-jnp.abs(pre)))) * (1.0 / GATE_TAU)
    hi = log_a.astype(BF16)
    lo = (log_a - hi.astype(F32)).astype(BF16)
    tri = tri_ref[...]
    suffix = (jnp.dot(tri, hi, preferred_element_type=F32)
              + jnp.dot(tri, lo, preferred_element_type=F32))
    kd = k_ref[...].astype(F32) * jnp.exp(suffix)
    row = lax.broadcasted_iota(jnp.int32, kd.shape, 0)
    first_half = ((row // CHUNK) % 2) == 0
    kd_pair = (jnp.where(first_half, kd, 0.0).astype(BF16),
               jnp.where(first_half, 0.0, kd).astype(BF16))
    vt = v_ref[...].astype(F32).T.astype(BF16)
    on = on_ref[...]
    scale = GLA_DK ** -0.5

    n_chunks = tm // CHUNK
    kcols = [slice(h * GLA_DK, (h + 1) * GLA_DK) for h in range(GLA_HEADS)]
    vcols = [slice(h * GLA_DV, (h + 1) * GLA_DV) for h in range(GLA_HEADS)]

    uts = []
    for c in range(n_chunks):
        p0 = (c // 2) * 2 * CHUNK
        uts.append([jnp.dot(vt[vcols[h], p0:p0 + 2 * CHUNK],
                            kd_pair[c % 2][p0:p0 + 2 * CHUNK, kcols[h]],
                            preferred_element_type=F32) for h in range(GLA_HEADS)])

    cur = [st_ref[h] for h in range(GLA_HEADS)]
    states = []
    for c in range(n_chunks):
        r0 = c * CHUNK
        decay = jnp.exp(jnp.sum(log_a[r0:r0 + CHUNK, :], axis=0, keepdims=True))
        cur = [cur[h] * decay[:, kcols[h]] + uts[c][h] for h in range(GLA_HEADS)]
        states.append([s.astype(BF16) for s in cur])
    for h in range(GLA_HEADS):
        st_ref[h] = cur[h]

    for c in range(n_chunks):
        r0 = c * CHUNK
        for h in range(GLA_HEADS):
            o = lax.dot_general(q_ref[r0:r0 + CHUNK, kcols[h]], states[c][h],
                                (((1,), (1,)), ((), ())), preferred_element_type=F32) * scale
            o = _rms(o, on)
            gate = g_ref[r0:r0 + CHUNK, vcols[h]].astype(F32)
            o_ref[r0:r0 + CHUNK, vcols[h]] = (o * (gate * jax.nn.sigmoid(gate))).astype(o_ref.dtype)


def _gla(z, wg, bg, on, tri, tm):
    s = z.shape[0]
    full = lambda a: pl.BlockSpec(a.shape, lambda i: (0, 0))
    return pl.pallas_call(
        functools.partial(_gla_kernel, tm=tm),
        grid=(s // tm,),
        in_specs=[
            pl.BlockSpec((tm, GLA_K_WIDTH), lambda i: (i, Z_Q // GLA_K_WIDTH)),
            pl.BlockSpec((tm, GLA_K_WIDTH), lambda i: (i, Z_K // GLA_K_WIDTH)),
            pl.BlockSpec((tm, GLA_WIDTH), lambda i: (i, Z_V // GLA_WIDTH)),
            pl.BlockSpec((tm, GLA_WIDTH), lambda i: (i, Z_G // GLA_WIDTH)),
            pl.BlockSpec((tm, LANES), lambda i: (i, Z_KRA // LANES)),
            full(wg), full(bg), full(on), full(tri),
        ],
        out_specs=pl.BlockSpec((tm, GLA_WIDTH), lambda i: (i, 0)),
        out_shape=jax.ShapeDtypeStruct((s, GLA_WIDTH), BF16),
        scratch_shapes=[pltpu.VMEM((GLA_HEADS, GLA_DV, GLA_DK), F32)],
        compiler_params=_cparams(("arbitrary",)),
        name="gla",
    )(z, z, z, z, z, wg, bg, on, tri)


def _out_proj_kernel(a_ref, g_ref, x_ref, an_ref, w_ref, o_ref):
    an = _rms(a_ref[...].astype(F32), an_ref[...]).astype(BF16)
    acc = jnp.dot(an, w_ref[:MLA_WIDTH, :], preferred_element_type=F32)
    acc = acc + jnp.dot(g_ref[...], w_ref[MLA_WIDTH:, :], preferred_element_type=F32)
    o_ref[...] = x_ref[...] + acc


def _out_proj(a, g, x, an, w, tm):
    s, d = x.shape
    return pl.pallas_call(
        _out_proj_kernel,
        grid=(s // tm,),
        in_specs=[
            pl.BlockSpec((tm, MLA_WIDTH), lambda i: (i, 0)),
            pl.BlockSpec((tm, GLA_WIDTH), lambda i: (i, 0)),
            pl.BlockSpec((tm, d), lambda i: (i, 0)),
            pl.BlockSpec((1, MLA_WIDTH), lambda i: (0, 0)),
            pl.BlockSpec((MIX_WIDTH, d), lambda i: (0, 0)),
        ],
        out_specs=pl.BlockSpec((tm, d), lambda i: (i, 0)),
        out_shape=jax.ShapeDtypeStruct((s, d), F32),
        compiler_params=_cparams(("arbitrary",)),
        name="out_proj",
    )(a, g, x, an, w)


HALO = 8
FIX = 16


def _ffn_kernel(x_ref, g_ref, wg_ref, wu_ref, cwg_ref, cwu_ref, cbg_ref, cbu_ref, wd_ref, o_ref,
                hn_ref, carry_ref, act_ref, *, tm):
    i = pl.program_id(0)
    f = pl.program_id(1)

    @pl.when(f == 0)
    def _():
        x = x_ref[...]
        hn_ref[...] = _rms(x, g_ref[...]).astype(BF16)
        o_ref[...] = x

    @pl.when(i == 0)
    def _():
        carry_ref[f] = jnp.zeros(carry_ref.shape[1:], F32)

    hn = hn_ref[...]

    def conv(u, cw_ref, cb_ref, slot):
        w0, w1, w2 = cw_ref[0:1, :], cw_ref[1:2, :], cw_ref[2:3, :]
        cb = cb_ref[...]
        y = cb + w0 * pltpu.roll(u, 2, 0) + w1 * pltpu.roll(u, 1, 0) + w2 * u
        ext = jnp.concatenate([carry_ref[f, slot], u[:FIX, :]], axis=0)
        y_fix = (cb + w0 * pltpu.roll(ext, 2, 0)[HALO:, :] + w1 * pltpu.roll(ext, 1, 0)[HALO:, :]
                 + w2 * ext[HALO:, :])
        carry_ref[f, slot] = u[tm - HALO:, :]
        return y, y_fix

    ug = jnp.dot(hn, wg_ref[...], preferred_element_type=F32)
    yg, yg_fix = conv(ug, cwg_ref, cbg_ref, 0)
    uu = jnp.dot(hn, wu_ref[...], preferred_element_type=F32)
    yu, yu_fix = conv(uu, cwu_ref, cbu_ref, 1)
    act_ref[...] = (yg * jax.nn.sigmoid(yg) * yu).astype(BF16)
    act_ref[0:FIX, :] = (yg_fix * jax.nn.sigmoid(yg_fix) * yu_fix).astype(BF16)
    o_ref[...] += jnp.dot(act_ref[...], wd_ref[...], preferred_element_type=F32)


def _ffn(x, gain, w_up, conv_w, conv_b, w_down, tm, tf):
    s, d = x.shape
    nf = D_FF // tf
    return pl.pallas_call(
        functools.partial(_ffn_kernel, tm=tm),
        grid=(s // tm, nf),
        in_specs=[
            pl.BlockSpec((tm, d), lambda i, f: (i, 0), pipeline_mode=pl.Buffered(1)),
            pl.BlockSpec((1, d), lambda i, f: (0, 0)),
            pl.BlockSpec((d, tf), lambda i, f: (0, f)),
            pl.BlockSpec((d, tf), lambda i, f: (0, f + nf)),
            pl.BlockSpec((CONV_W, tf), lambda i, f: (0, f)),
            pl.BlockSpec((CONV_W, tf), lambda i, f: (0, f + nf)),
            pl.BlockSpec((1, tf), lambda i, f: (0, f)),
            pl.BlockSpec((1, tf), lambda i, f: (0, f + nf)),
            pl.BlockSpec((tf, d), lambda i, f: (f, 0)),
        ],
        out_specs=pl.BlockSpec((tm, d), lambda i, f: (i, 0)),
        out_shape=jax.ShapeDtypeStruct((s, d), F32),
        scratch_shapes=[
            pltpu.VMEM((tm, d), BF16),
            pltpu.VMEM((nf, 2, HALO, tf), F32),
            pltpu.VMEM((tm, tf), BF16),
        ],
        compiler_params=_cparams(("arbitrary", "arbitrary")),
        name="conv_ffn",
    )(x, gain, w_up, w_up, conv_w, conv_w, conv_b, conv_b, w_down)


def _pack_w_in(w):
    d = w.shape[0]
    o_cq, o_ckv, o_kr = 0, Q_LORA, Q_LORA + KV_LORA
    o_q = o_kr + ROPE_DIM
    o_k = o_q + GLA_K_WIDTH
    o_v = o_k + GLA_K_WIDTH
    o_a = o_v + GLA_WIDTH
    o_g = o_a + GATE_RANK
    zeros = lambda n: jnp.zeros((d, n), w.dtype)
    kra = jnp.concatenate([
        w[:, o_kr:o_kr + ROPE_HALF], w[:, o_a:o_a + GATE_RANK], zeros(KRA_X2_OFF - KRA_A_OFF - GATE_RANK),
        w[:, o_kr + ROPE_HALF:o_kr + ROPE_DIM], zeros(LANES - KRA_X2_OFF - ROPE_HALF)], axis=1)
    packed = jnp.concatenate([
        w[:, o_cq:o_cq + Q_LORA], w[:, o_q:o_q + GLA_K_WIDTH], w[:, o_k:o_k + GLA_K_WIDTH],
        w[:, o_ckv:o_ckv + KV_LORA], kra, zeros(Z_V - Z_KRA - LANES),
        w[:, o_v:o_v + GLA_WIDTH], w[:, o_g:o_g + GLA_WIDTH]], axis=1)
    return packed.astype(BF16)


def _pack_head_vec(g):
    z = jnp.zeros((ROPE_HALF,), g.dtype)
    return jnp.concatenate([g[:NOPE_DIM], g[NOPE_DIM:NOPE_DIM + ROPE_HALF], z,
                            g[NOPE_DIM + ROPE_HALF:], z])[None, :]


def _pack_w_uq(w):
    r = w.shape[0]
    w = w.reshape(r, MLA_HEADS, QK_HEAD)
    z = jnp.zeros((r, MLA_HEADS, ROPE_HALF), w.dtype)
    w = jnp.concatenate([w[..., :NOPE_DIM], w[..., NOPE_DIM:NOPE_DIM + ROPE_HALF], z,
                         w[..., NOPE_DIM + ROPE_HALF:], z], axis=-1)
    return w.reshape(r, MLA_HEADS * HEAD_PAD).astype(BF16)


def _pack_w_ukv(w):
    r = w.shape[0]
    w = w.reshape(r, MLA_HEADS, NOPE_DIM + MLA_V_DIM)
    return jnp.concatenate([w[..., :NOPE_DIM].reshape(r, -1), w[..., NOPE_DIM:].reshape(r, -1)],
                           axis=1).astype(BF16)


def _pack_w_gate(w):
    z = lambda n: jnp.zeros((n, w.shape[1]), w.dtype)
    return jnp.concatenate([z(KRA_A_OFF), w, z(LANES - KRA_A_OFF - GATE_RANK)], axis=0).astype(BF16)


def _rope_tables(positions):
    inv_freq = ROPE_BASE ** (-jnp.arange(0, ROPE_DIM, 2, dtype=F32) / ROPE_DIM)
    angle = positions.astype(F32)[:, None] * inv_freq
    cos, sin = jnp.cos(angle), jnp.sin(angle)
    z = jnp.zeros_like(cos)
    return (jnp.concatenate([cos, z, cos, z], axis=1), jnp.concatenate([-sin, z, sin, z], axis=1))


def _suffix_matrix(tm):
    t = jnp.arange(tm)
    same = (t[:, None] // CHUNK) == (t[None, :] // CHUNK)
    return (same & (t[None, :] > t[:, None])).astype(BF16)


def kernel(x, positions, attn_norm, w_in, cq_norm, w_uq, ckv_norm, w_ukv, q_norm, k_norm, w_gate,
           b_gate, gla_out_norm, mla_out_norm, w_out, ffn_norm, w_up, conv_w, conv_b, w_down):
    batch, seq, d = x.shape
    assert batch == 1 and d == D_MODEL
    tm_proj = min(1024, seq)
    tm = min(512, seq)
    tm_gla = min(256, seq)
    ct, st = _rope_tables(positions[0])
    tri = _suffix_matrix(tm_gla)
    q_scale = math.log2(math.e) / math.sqrt(QK_HEAD)
    h = x[0]
    for l in range(DEPTH):
        z = _in_proj(h, attn_norm[l][None, :], _pack_w_in(w_in[l]), tm_proj, 1024)
        qg = _pack_head_vec(q_norm[l]) * q_scale
        kg = _pack_head_vec(k_norm[l])
        q, k, v = _mla_prep(z, ct, st, cq_norm[l][None, :], ckv_norm[l][None, :],
                            _pack_w_uq(w_uq[l]), _pack_w_ukv(w_ukv[l]), qg, kg, tm)
        bound = SCORE_BOUND_MARGIN * QK_HEAD * jnp.max(jnp.abs(qg)) * jnp.max(jnp.abs(kg))
        a = lax.cond(
            bound <= SOFTMAX_SHIFT_LIMIT,
            lambda: _attention_shifted(q, k, v, jnp.full((1, 2 * tm), bound, F32), tm),
            lambda: _attention(q, k, v, tm))
        g = _gla(z, _pack_w_gate(w_gate[l]), b_gate[l][None, :], gla_out_norm[l][None, :], tri,
                 tm_gla)
        h = _out_proj(a, g, h, mla_out_norm[l][None, :], w_out[l].astype(BF16), tm)
        h = _ffn(h, ffn_norm[l][None, :], w_up[l].astype(BF16), conv_w[l], conv_b[l][None, :],
                 w_down[l].astype(BF16), tm_proj, 512)
    return h[None]
```

```python
import functools
import math

import jax
import jax.numpy as jnp
from jax import lax
from jax.experimental import pallas as pl
from jax.experimental.pallas import tpu as pltpu

F32 = jnp.float32
BF16 = jnp.bfloat16

D_MODEL = 2048
DEPTH = 2
CHUNK = 64
MLA_HEADS = 8
Q_LORA = 512
KV_LORA = 256
NOPE_DIM = 128
ROPE_DIM = 64
ROPE_HALF = ROPE_DIM // 2
QK_HEAD = NOPE_DIM + ROPE_DIM
MLA_V_DIM = 128
MLA_WIDTH = MLA_HEADS * MLA_V_DIM
ROPE_BASE = 10000.0
GLA_HEADS = 4
GLA_DK = 128
GLA_DV = 256
GLA_K_WIDTH = GLA_HEADS * GLA_DK
GLA_WIDTH = GLA_HEADS * GLA_DV
GATE_RANK = 16
GATE_TAU = 16.0
MIX_WIDTH = MLA_WIDTH + GLA_WIDTH
D_FF = 5632
CONV_W = 3
NORM_EPS = 1e-6

LANES = 128
HEAD_PAD = 2 * LANES

Z_Q, Z_K, Z_V, Z_G, Z_CQ, Z_CKV, Z_KR, Z_A = 0, 512, 1024, 2048, 3072, 3584, 3840, 3968
Z_WIDTH = 4096
KRA_X2_OFF = 2 * ROPE_HALF

VMEM_LIMIT = 56 * 1024 * 1024

SOFTMAX_SHIFT_LIMIT = 60.0
SCORE_BOUND_MARGIN = 1.02


def _cparams(sem):
    return pltpu.CompilerParams(dimension_semantics=sem, vmem_limit_bytes=VMEM_LIMIT)


def _rms(x, gain):
    ms = jnp.mean(x * x, axis=-1, keepdims=True)
    return x * lax.rsqrt(ms + NORM_EPS) * gain


def _in_proj_kernel(x_ref, g_ref, w_ref, o_ref, hn_ref):
    @pl.when(pl.program_id(1) == 0)
    def _():
        hn_ref[...] = _rms(x_ref[...], g_ref[...]).astype(BF16)

    o_ref[...] = jnp.dot(hn_ref[...], w_ref[...], preferred_element_type=F32).astype(o_ref.dtype)


def _in_proj(x, gain, w, tm, tn):
    s, d = x.shape
    n = w.shape[1]
    return pl.pallas_call(
        _in_proj_kernel,
        grid=(s // tm, n // tn),
        in_specs=[
            pl.BlockSpec((tm, d), lambda i, j: (i, 0)),
            pl.BlockSpec((1, d), lambda i, j: (0, 0)),
            pl.BlockSpec((d, tn), lambda i, j: (0, j)),
        ],
        out_specs=pl.BlockSpec((tm, tn), lambda i, j: (i, j)),
        out_shape=jax.ShapeDtypeStruct((s, n), BF16),
        scratch_shapes=[pltpu.VMEM((tm, d), BF16)],
        compiler_params=_cparams(("arbitrary", "arbitrary")),
        name="in_proj",
    )(x, gain, w)


def _rope(b, ct, st):
    return b * ct + pltpu.roll(b, KRA_X2_OFF, 1) * st


def _mla_prep_kernel(cq_ref, ckv_ref, kr_ref, ct_ref, st_ref, cost_ref, sint_ref, cqg_ref, ckvg_ref,
                     wuqt_ref, wkn_ref, wvt_ref, qgt_ref, kg_ref, qt_ref, k_ref, vt_ref):
    inv_head = 1.0 / QK_HEAD
    tm = cq_ref.shape[0]

    cqn_t = _rms(cq_ref[...].astype(F32), cqg_ref[...]).T.astype(BF16)
    q_t = jnp.dot(wuqt_ref[...], cqn_t, preferred_element_type=F32)
    qg_t = jnp.concatenate([qgt_ref[...]] * (tm // LANES), axis=1)
    cos_t = cost_ref[...]
    sin_t = sint_ref[...]
    zeros = jnp.zeros((ROPE_HALF, tm), F32)
    x1_rows = slice(NOPE_DIM, NOPE_DIM + ROPE_HALF)
    x2_rows = slice(NOPE_DIM + KRA_X2_OFF, NOPE_DIM + KRA_X2_OFF + ROPE_HALF)
    for h in range(MLA_HEADS):
        qh = q_t[h * HEAD_PAD:(h + 1) * HEAD_PAD, :]
        ssq = jnp.sum(qh * qh, axis=0, keepdims=True)
        qh = qh * lax.rsqrt(ssq * inv_head + NORM_EPS) * qg_t
        x1, x2 = qh[x1_rows, :], qh[x2_rows, :]
        qt_ref[h, :NOPE_DIM, :] = qh[:NOPE_DIM, :].astype(BF16)
        qt_ref[h, NOPE_DIM:, :] = jnp.concatenate(
            [x1 * cos_t - x2 * sin_t, zeros, x2 * cos_t + x1 * sin_t, zeros], axis=0).astype(BF16)

    ckvn = _rms(ckv_ref[...].astype(F32), ckvg_ref[...])
    v_t = jnp.dot(wvt_ref[...], ckvn.T.astype(BF16), preferred_element_type=F32)
    for h in range(MLA_HEADS):
        vt_ref[h, 0] = v_t[h * MLA_V_DIM:(h + 1) * MLA_V_DIM, :].astype(BF16)

    kn_all = jnp.dot(ckvn.astype(BF16), wkn_ref[...], preferred_element_type=F32)
    kg = kg_ref[...]
    kg_n = kg[:, :LANES]
    kg_r = kg[:, LANES:]
    kr = kr_ref[...].astype(F32)
    ssq_r = jnp.sum(kr * kr, axis=-1, keepdims=True)
    kr_rot = _rope(kr * kg_r, ct_ref[...], st_ref[...])
    for h in range(MLA_HEADS):
        kn = kn_all[:, h * NOPE_DIM:(h + 1) * NOPE_DIM]
        ssq = jnp.sum(kn * kn, axis=-1, keepdims=True) + ssq_r
        r = lax.rsqrt(ssq * inv_head + NORM_EPS)
        k_ref[:, h * HEAD_PAD:h * HEAD_PAD + LANES] = (kn * r * kg_n).astype(BF16)
        k_ref[:, h * HEAD_PAD + LANES:(h + 1) * HEAD_PAD] = (kr_rot * r).astype(BF16)


def _mla_prep(z, ct, st, cos_t, sin_t, cqg, ckvg, wuqt, wkn, wvt, qgt, kg, tm):
    s = z.shape[0]
    full = lambda a: pl.BlockSpec(a.shape, lambda i: (0, 0))
    return pl.pallas_call(
        _mla_prep_kernel,
        grid=(s // tm,),
        in_specs=[
            pl.BlockSpec((tm, Q_LORA), lambda i: (i, Z_CQ // Q_LORA)),
            pl.BlockSpec((tm, KV_LORA), lambda i: (i, Z_CKV // KV_LORA)),
            pl.BlockSpec((tm, LANES), lambda i: (i, Z_KR // LANES)),
            pl.BlockSpec((tm, LANES), lambda i: (i, 0)),
            pl.BlockSpec((tm, LANES), lambda i: (i, 0)),
            pl.BlockSpec((ROPE_HALF, tm), lambda i: (0, i)),
            pl.BlockSpec((ROPE_HALF, tm), lambda i: (0, i)),
            full(cqg), full(ckvg), full(wuqt), full(wkn), full(wvt), full(qgt), full(kg),
        ],
        out_specs=[
            pl.BlockSpec((MLA_HEADS, HEAD_PAD, tm), lambda i: (0, 0, i)),
            pl.BlockSpec((tm, MLA_HEADS * HEAD_PAD), lambda i: (i, 0)),
            pl.BlockSpec((MLA_HEADS, 1, MLA_V_DIM, tm), lambda i: (0, i, 0, 0)),
        ],
        out_shape=[
            jax.ShapeDtypeStruct((MLA_HEADS, HEAD_PAD, s), BF16),
            jax.ShapeDtypeStruct((s, MLA_HEADS * HEAD_PAD), BF16),
            jax.ShapeDtypeStruct((MLA_HEADS, s // tm, MLA_V_DIM, tm), BF16),
        ],
        compiler_params=_cparams(("arbitrary",)),
        name="mla_prep",
    )(z, z, z, ct, st, cos_t, sin_t, cqg, ckvg, wuqt, wkn, wvt, qgt, kg)


def _attn_kernel(qt_ref, k_ref, vt_ref, o_ref, sa_ref, sb_ref, m_ref, l_ref, acc_ref, *, tk):
    qi = pl.program_id(1)
    m_ref[...] = jnp.full(m_ref.shape, -1e30, F32)
    l_ref[...] = jnp.zeros(l_ref.shape, F32)
    acc_ref[...] = jnp.zeros(acc_ref.shape, F32)

    def scores(kj, s_ref, lo):
        start = pl.multiple_of(kj * tk, tk)
        s_ref[:, lo:] = jnp.dot(k_ref[pl.ds(start, tk), :], qt_ref[:, lo:],
                                preferred_element_type=F32)

    def update(kj, s_ref, lo, masked):
        s = s_ref[:, lo:]
        if masked:
            kc = lax.broadcasted_iota(jnp.int32, s.shape, 0) // CHUNK
            qc = lax.broadcasted_iota(jnp.int32, s.shape, 1) // CHUNK
            s = jnp.where(kc <= qc, s, -1e30)
        m_old = m_ref[:, lo:]
        m_new = jnp.maximum(m_old, jnp.max(s, axis=0, keepdims=True))
        alpha = jnp.exp2(m_old - m_new)
        p = jnp.exp2(s - m_new)
        l_ref[:, lo:] = alpha * l_ref[:, lo:] + jnp.sum(p, axis=0, keepdims=True)
        acc_ref[:, lo:] = alpha * acc_ref[:, lo:] + jnp.dot(vt_ref[kj], p.astype(BF16),
                                                            preferred_element_type=F32)
        m_ref[:, lo:] = m_new

    scores(0, sa_ref, 0)

    def body(i, carry):
        scores(2 * i + 1, sb_ref, 0)
        update(2 * i, sa_ref, 0, False)
        scores(2 * i + 2, sa_ref, 0)
        update(2 * i + 1, sb_ref, 0, False)
        return carry

    lax.fori_loop(0, qi, body, 0)
    scores(2 * qi + 1, sb_ref, tk)
    update(2 * qi, sa_ref, 0, True)
    update(2 * qi + 1, sb_ref, tk, True)
    o_ref[...] = (acc_ref[...] / l_ref[...]).T.astype(o_ref.dtype)


def _attn_shifted_kernel(qt_ref, k_ref, vt_ref, shift_ref, o_ref, l_ref, acc_ref, *, tk):
    qi = pl.program_id(1)
    l_ref[...] = jnp.zeros(l_ref.shape, F32)
    acc_ref[...] = jnp.zeros(acc_ref.shape, F32)

    def tile(kj, lo, masked):
        start = pl.multiple_of(kj * tk, tk)
        s = jnp.dot(k_ref[pl.ds(start, tk), :], qt_ref[:, lo:], preferred_element_type=F32)
        p = jnp.exp2(s - shift_ref[:, lo:])
        if masked:
            kc = lax.broadcasted_iota(jnp.int32, p.shape, 0) // CHUNK
            qc = lax.broadcasted_iota(jnp.int32, p.shape, 1) // CHUNK
            p = jnp.where(kc <= qc, p, 0.0)
        l_ref[:, lo:] += jnp.sum(p, axis=0, keepdims=True)
        acc_ref[:, lo:] += jnp.dot(vt_ref[kj], p.astype(BF16), preferred_element_type=F32)

    def body(i, carry):
        for u in range(4):
            tile(4 * i + u, 0, False)
        return carry

    lax.fori_loop(0, qi // 2, body, 0)

    @pl.when(qi % 2 == 1)
    def _():
        tile(2 * qi - 2, 0, False)
        tile(2 * qi - 1, 0, False)

    tile(2 * qi, 0, True)
    tile(2 * qi + 1, tk, True)
    o_ref[...] = (acc_ref[...] / l_ref[...]).T.astype(o_ref.dtype)


def _attention_shifted(qt, k, vt, shift, tk):
    s = k.shape[0]
    tq = 2 * tk
    return pl.pallas_call(
        functools.partial(_attn_shifted_kernel, tk=tk),
        grid=(MLA_HEADS, s // tq),
        in_specs=[
            pl.BlockSpec((None, HEAD_PAD, tq), lambda h, i: (h, 0, i)),
            pl.BlockSpec((s, HEAD_PAD), lambda h, i: (0, h)),
            pl.BlockSpec((None, s // tk, MLA_V_DIM, tk), lambda h, i: (h, 0, 0, 0)),
            pl.BlockSpec((1, tq), lambda h, i: (0, 0)),
        ],
        out_specs=pl.BlockSpec((tq, MLA_V_DIM), lambda h, i: (i, h)),
        out_shape=jax.ShapeDtypeStruct((s, MLA_WIDTH), BF16),
        scratch_shapes=[
            pltpu.VMEM((1, tq), F32),
            pltpu.VMEM((MLA_V_DIM, tq), F32),
        ],
        compiler_params=_cparams(("arbitrary", "arbitrary")),
        name="mla_attention_shifted",
    )(qt, k, vt, shift)


def _attention(qt, k, vt, tk):
    s = k.shape[0]
    tq = 2 * tk
    return pl.pallas_call(
        functools.partial(_attn_kernel, tk=tk),
        grid=(MLA_HEADS, s // tq),
        in_specs=[
            pl.BlockSpec((None, HEAD_PAD, tq), lambda h, i: (h, 0, i)),
            pl.BlockSpec((s, HEAD_PAD), lambda h, i: (0, h)),
            pl.BlockSpec((None, s // tk, MLA_V_DIM, tk), lambda h, i: (h, 0, 0, 0)),
        ],
        out_specs=pl.BlockSpec((tq, MLA_V_DIM), lambda h, i: (i, h)),
        out_shape=jax.ShapeDtypeStruct((s, MLA_WIDTH), BF16),
        scratch_shapes=[
            pltpu.VMEM((tk, tq), F32),
            pltpu.VMEM((tk, tq), F32),
            pltpu.VMEM((1, tq), F32),
            pltpu.VMEM((1, tq), F32),
            pltpu.VMEM((MLA_V_DIM, tq), F32),
        ],
        compiler_params=_cparams(("arbitrary", "arbitrary")),
        name="mla_attention",
    )(qt, k, vt)


def _gla_kernel(q_ref, k_ref, v_ref, g_ref, kra_ref, wg_ref, bg_ref, on_ref, tri_ref, o_ref,
                st_ref, *, tm):
    @pl.when(pl.program_id(0) == 0)
    def _():
        st_ref[...] = jnp.zeros(st_ref.shape, F32)

    pre = jnp.dot(kra_ref[...], wg_ref[...], preferred_element_type=F32) + bg_ref[...]
    log_a = (jnp.minimum(pre, 0.0) - jnp.log1p(jnp.exp(-jnp.abs(pre)))) * (1.0 / GATE_TAU)
    hi = log_a.astype(BF16)
    lo = (log_a - hi.astype(F32)).astype(BF16)
    tri = tri_ref[...]
    suffix = (jnp.dot(tri, hi, preferred_element_type=F32)
              + jnp.dot(tri, lo, preferred_element_type=F32))
    kd = k_ref[...].astype(F32) * jnp.exp(suffix)
    row = lax.broadcasted_iota(jnp.int32, kd.shape, 0)
    first_half = ((row // CHUNK) % 2) == 0
    kd_pair = (jnp.where(first_half, kd, 0.0).astype(BF16),
               jnp.where(first_half, 0.0, kd).astype(BF16))
    vt = v_ref[...].astype(F32).T.astype(BF16)
    on = on_ref[...]
    scale = GLA_DK ** -0.5

    n_chunks = tm // CHUNK
    kcols = [slice(h * GLA_DK, (h + 1) * GLA_DK) for h in range(GLA_HEADS)]
    vcols = [slice(h * GLA_DV, (h + 1) * GLA_DV) for h in range(GLA_HEADS)]

    uts = []
    for c in range(n_chunks):
        p0 = (c // 2) * 2 * CHUNK
        uts.append([jnp.dot(vt[vcols[h], p0:p0 + 2 * CHUNK],
                            kd_pair[c % 2][p0:p0 + 2 * CHUNK, kcols[h]],
                            preferred_element_type=F32) for h in range(GLA_HEADS)])

    cur = [st_ref[h] for h in range(GLA_HEADS)]
    states = []
    for c in range(n_chunks):
        r0 = c * CHUNK
        decay = jnp.exp(jnp.sum(log_a[r0:r0 + CHUNK, :], axis=0, keepdims=True))
        cur = [cur[h] * decay[:, kcols[h]] + uts[c][h] for h in range(GLA_HEADS)]
        states.append([s.astype(BF16) for s in cur])
    for h in range(GLA_HEADS):
        st_ref[h] = cur[h]

    for c in range(n_chunks):
        r0 = c * CHUNK
        for h in range(GLA_HEADS):
            o = lax.dot_general(q_ref[r0:r0 + CHUNK, kcols[h]], states[c][h],
                                (((1,), (1,)), ((), ())), preferred_element_type=F32) * scale
            o = _rms(o, on)
            gate = g_ref[r0:r0 + CHUNK, vcols[h]].astype(F32)
            o_ref[r0:r0 + CHUNK, vcols[h]] = (o * (gate * jax.nn.sigmoid(gate))).astype(o_ref.dtype)


def _gla(z, wg, bg, on, tri, tm):
    s = z.shape[0]
    full = lambda a: pl.BlockSpec(a.shape, lambda i: (0, 0))
    return pl.pallas_call(
        functools.partial(_gla_kernel, tm=tm),
        grid=(s // tm,),
        in_specs=[
            pl.BlockSpec((tm, GLA_K_WIDTH), lambda i: (i, Z_Q // GLA_K_WIDTH)),
            pl.BlockSpec((tm, GLA_K_WIDTH), lambda i: (i, Z_K // GLA_K_WIDTH)),
            pl.BlockSpec((tm, GLA_WIDTH), lambda i: (i, Z_V // GLA_WIDTH)),
            pl.BlockSpec((tm, GLA_WIDTH), lambda i: (i, Z_G // GLA_WIDTH)),
            pl.BlockSpec((tm, LANES), lambda i: (i, Z_A // LANES)),
            full(wg), full(bg), full(on), full(tri),
        ],
        out_specs=pl.BlockSpec((tm, GLA_WIDTH), lambda i: (i, 0)),
        out_shape=jax.ShapeDtypeStruct((s, GLA_WIDTH), BF16),
        scratch_shapes=[pltpu.VMEM((GLA_HEADS, GLA_DV, GLA_DK), F32)],
        compiler_params=_cparams(("arbitrary",)),
        name="gla",
    )(z, z, z, z, z, wg, bg, on, tri)


def _out_proj_kernel(a_ref, g_ref, x_ref, an_ref, fn_ref, w_ref, o_ref, hn_ref):
    an = _rms(a_ref[...].astype(F32), an_ref[...]).astype(BF16)
    acc = jnp.dot(an, w_ref[:MLA_WIDTH, :], preferred_element_type=F32)
    acc = acc + jnp.dot(g_ref[...], w_ref[MLA_WIDTH:, :], preferred_element_type=F32)
    x = x_ref[...] + acc
    o_ref[...] = x
    hn_ref[...] = _rms(x, fn_ref[...]).astype(BF16)


def _out_proj(a, g, x, an, fn, w, tm):
    s, d = x.shape
    return pl.pallas_call(
        _out_proj_kernel,
        grid=(s // tm,),
        in_specs=[
            pl.BlockSpec((tm, MLA_WIDTH), lambda i: (i, 0)),
            pl.BlockSpec((tm, GLA_WIDTH), lambda i: (i, 0)),
            pl.BlockSpec((tm, d), lambda i: (i, 0)),
            pl.BlockSpec((1, MLA_WIDTH), lambda i: (0, 0)),
            pl.BlockSpec((1, d), lambda i: (0, 0)),
            pl.BlockSpec((MIX_WIDTH, d), lambda i: (0, 0)),
        ],
        out_specs=[pl.BlockSpec((tm, d), lambda i: (i, 0)), pl.BlockSpec((tm, d), lambda i: (i, 0))],
        out_shape=[jax.ShapeDtypeStruct((s, d), F32), jax.ShapeDtypeStruct((s, d), BF16)],
        compiler_params=_cparams(("arbitrary",)),
        name="out_proj",
    )(a, g, x, an, fn, w)


HALO = 8
FIX = 16


def _ffn_kernel(x_ref, hn_ref, wg_ref, wu_ref, cwg_ref, cwu_ref, cbg_ref, cbu_ref, wd_ref, o_ref,
                carry_ref, act_ref, *, tm):
    i = pl.program_id(0)
    f = pl.program_id(1)

    @pl.when(f == 0)
    def _():
        o_ref[...] = x_ref[...]

    @pl.when(i == 0)
    def _():
        carry_ref[f] = jnp.zeros(carry_ref.shape[1:], F32)

    hn = hn_ref[...]

    def conv(u, cw_ref, cb_ref, slot):
        w0, w1, w2 = cw_ref[0:1, :], cw_ref[1:2, :], cw_ref[2:3, :]
        cb = cb_ref[...]
        y = cb + w0 * pltpu.roll(u, 2, 0) + w1 * pltpu.roll(u, 1, 0) + w2 * u
        ext = jnp.concatenate([carry_ref[f, slot], u[:FIX, :]], axis=0)
        y_fix = (cb + w0 * pltpu.roll(ext, 2, 0)[HALO:, :] + w1 * pltpu.roll(ext, 1, 0)[HALO:, :]
                 + w2 * ext[HALO:, :])
        carry_ref[f, slot] = u[tm - HALO:, :]
        return y, y_fix

    ug = jnp.dot(hn, wg_ref[...], preferred_element_type=F32)
    yg, yg_fix = conv(ug, cwg_ref, cbg_ref, 0)
    uu = jnp.dot(hn, wu_ref[...], preferred_element_type=F32)
    yu, yu_fix = conv(uu, cwu_ref, cbu_ref, 1)
    act_ref[...] = (yg * jax.nn.sigmoid(yg) * yu).astype(BF16)
    act_ref[0:FIX, :] = (yg_fix * jax.nn.sigmoid(yg_fix) * yu_fix).astype(BF16)
    o_ref[...] += jnp.dot(act_ref[...], wd_ref[...], preferred_element_type=F32)


def _ffn(x, hn, w_up, conv_w, conv_b, w_down, tm, tf):
    s, d = x.shape
    nf = D_FF // tf
    return pl.pallas_call(
        functools.partial(_ffn_kernel, tm=tm),
        grid=(s // tm, nf),
        in_specs=[
            pl.BlockSpec((tm, d), lambda i, f: (i, 0), pipeline_mode=pl.Buffered(1)),
            pl.BlockSpec((tm, d), lambda i, f: (i, 0)),
            pl.BlockSpec((d, tf), lambda i, f: (0, f)),
            pl.BlockSpec((d, tf), lambda i, f: (0, f + nf)),
            pl.BlockSpec((CONV_W, tf), lambda i, f: (0, f)),
            pl.BlockSpec((CONV_W, tf), lambda i, f: (0, f + nf)),
            pl.BlockSpec((1, tf), lambda i, f: (0, f)),
            pl.BlockSpec((1, tf), lambda i, f: (0, f + nf)),
            pl.BlockSpec((tf, d), lambda i, f: (f, 0)),
        ],
        out_specs=pl.BlockSpec((tm, d), lambda i, f: (i, 0)),
        out_shape=jax.ShapeDtypeStruct((s, d), F32),
        scratch_shapes=[
            pltpu.VMEM((nf, 2, HALO, tf), F32),
            pltpu.VMEM((tm, tf), BF16),
        ],
        compiler_params=_cparams(("arbitrary", "arbitrary")),
        name="conv_ffn",
    )(x, hn, w_up, w_up, conv_w, conv_w, conv_b, conv_b, w_down)


def _pack_w_in(w):
    d = w.shape[0]
    o_cq, o_ckv, o_kr = 0, Q_LORA, Q_LORA + KV_LORA
    o_q = o_kr + ROPE_DIM
    o_k = o_q + GLA_K_WIDTH
    o_v = o_k + GLA_K_WIDTH
    o_a = o_v + GLA_WIDTH
    o_g = o_a + GATE_RANK
    zeros = lambda n: jnp.zeros((d, n), w.dtype)
    kr = jnp.concatenate([
        w[:, o_kr:o_kr + ROPE_HALF], zeros(KRA_X2_OFF - ROPE_HALF),
        w[:, o_kr + ROPE_HALF:o_kr + ROPE_DIM], zeros(LANES - KRA_X2_OFF - ROPE_HALF)], axis=1)
    packed = jnp.concatenate([
        w[:, o_q:o_a], w[:, o_g:o_g + GLA_WIDTH], w[:, o_cq:o_kr], kr,
        w[:, o_a:o_a + GATE_RANK], zeros(LANES - GATE_RANK)], axis=1)
    return packed.astype(BF16)


def _pack_head_vec(g):
    z = jnp.zeros((ROPE_HALF,), g.dtype)
    return jnp.concatenate([g[:NOPE_DIM], g[NOPE_DIM:NOPE_DIM + ROPE_HALF], z,
                            g[NOPE_DIM + ROPE_HALF:], z])[None, :]


def _pack_w_uq(w):
    r = w.shape[0]
    w = w.reshape(r, MLA_HEADS, QK_HEAD)
    z = jnp.zeros((r, MLA_HEADS, ROPE_HALF), w.dtype)
    w = jnp.concatenate([w[..., :NOPE_DIM], w[..., NOPE_DIM:NOPE_DIM + ROPE_HALF], z,
                         w[..., NOPE_DIM + ROPE_HALF:], z], axis=-1)
    return w.reshape(r, MLA_HEADS * HEAD_PAD).astype(BF16)


def _pack_w_ukv(w):
    r = w.shape[0]
    w = w.reshape(r, MLA_HEADS, NOPE_DIM + MLA_V_DIM)
    return (w[..., :NOPE_DIM].reshape(r, -1).astype(BF16),
            w[..., NOPE_DIM:].reshape(r, -1).T.astype(BF16))


def _pack_w_gate(w):
    z = jnp.zeros((LANES - GATE_RANK, w.shape[1]), w.dtype)
    return jnp.concatenate([w, z], axis=0).astype(BF16)


def _rope_tables(positions):
    inv_freq = ROPE_BASE ** (-jnp.arange(0, ROPE_DIM, 2, dtype=F32) / ROPE_DIM)
    angle = positions.astype(F32)[:, None] * inv_freq
    cos, sin = jnp.cos(angle), jnp.sin(angle)
    z = jnp.zeros_like(cos)
    return (jnp.concatenate([cos, z, cos, z], axis=1), jnp.concatenate([-sin, z, sin, z], axis=1),
            cos.T, sin.T)


def _suffix_matrix(tm):
    t = jnp.arange(tm)
    same = (t[:, None] // CHUNK) == (t[None, :] // CHUNK)
    return (same & (t[None, :] > t[:, None])).astype(BF16)


def kernel(x, positions, attn_norm, w_in, cq_norm, w_uq, ckv_norm, w_ukv, q_norm, k_norm, w_gate,
           b_gate, gla_out_norm, mla_out_norm, w_out, ffn_norm, w_up, conv_w, conv_b, w_down):
    batch, seq, d = x.shape
    assert batch == 1 and d == D_MODEL
    tm_proj = min(1024, seq)
    tm = min(512, seq)
    tm_gla = min(512, seq)
    ct, st, cos_t, sin_t = _rope_tables(positions[0])
    tri = _suffix_matrix(tm_gla)
    q_scale = math.log2(math.e) / math.sqrt(QK_HEAD)
    h = x[0]
    for l in range(DEPTH):
        z = _in_proj(h, attn_norm[l][None, :], _pack_w_in(w_in[l]), tm_proj, 1024)
        qg = _pack_head_vec(q_norm[l]) * q_scale
        kg = _pack_head_vec(k_norm[l])
        w_kn, w_vt = _pack_w_ukv(w_ukv[l])
        q, k, v = _mla_prep(z, ct, st, cos_t, sin_t, cq_norm[l][None, :], ckv_norm[l][None, :],
                            _pack_w_uq(w_uq[l]).T, w_kn, w_vt,
                            jnp.broadcast_to(qg.T, (HEAD_PAD, LANES)), kg, tm)
        bound = SCORE_BOUND_MARGIN * QK_HEAD * jnp.max(jnp.abs(qg)) * jnp.max(jnp.abs(kg))
        a = lax.cond(
            bound <= SOFTMAX_SHIFT_LIMIT,
            lambda: _attention_shifted(q, k, v, jnp.full((1, 2 * tm), bound, F32), tm),
            lambda: _attention(q, k, v, tm))
        g = _gla(z, _pack_w_gate(w_gate[l]), b_gate[l][None, :], gla_out_norm[l][None, :], tri,
                 tm_gla)
        h, hn = _out_proj(a, g, h, mla_out_norm[l][None, :], ffn_norm[l][None, :],
                          w_out[l].astype(BF16), tm)
        h = _ffn(h, hn, w_up[l].astype(BF16), conv_w[l], conv_b[l][None, :],
                 w_down[l].astype(BF16), tm_proj, 512)
    return h[None]
```

```python
import functools
import math

import jax
import jax.numpy as jnp
from jax import lax
from jax.experimental import pallas as pl
from jax.experimental.pallas import tpu as pltpu

F32 = jnp.float32
BF16 = jnp.bfloat16

D_MODEL = 2048
DEPTH = 2
CHUNK = 64
MLA_HEADS = 8
Q_LORA = 512
KV_LORA = 256
NOPE_DIM = 128
ROPE_DIM = 64
ROPE_HALF = ROPE_DIM // 2
QK_HEAD = NOPE_DIM + ROPE_DIM
MLA_V_DIM = 128
MLA_WIDTH = MLA_HEADS * MLA_V_DIM
ROPE_BASE = 10000.0
GLA_HEADS = 4
GLA_DK = 128
GLA_DV = 256
GLA_K_WIDTH = GLA_HEADS * GLA_DK
GLA_WIDTH = GLA_HEADS * GLA_DV
GATE_RANK = 16
GATE_TAU = 16.0
MIX_WIDTH = MLA_WIDTH + GLA_WIDTH
D_FF = 5632
CONV_W = 3
NORM_EPS = 1e-6

LANES = 128
HEAD_PAD = 2 * LANES

Z_Q, Z_K, Z_V, Z_G, Z_CQ, Z_CKV, Z_KR, Z_A = 0, 512, 1024, 2048, 3072, 3584, 3840, 3968
Z_WIDTH = 4096
KRA_X2_OFF = 2 * ROPE_HALF

VMEM_LIMIT = 56 * 1024 * 1024

SOFTMAX_SHIFT_LIMIT = 60.0
SCORE_BOUND_MARGIN = 1.02


def _cparams(sem):
    return pltpu.CompilerParams(dimension_semantics=sem, vmem_limit_bytes=VMEM_LIMIT)


def _rms(x, gain):
    ms = jnp.mean(x * x, axis=-1, keepdims=True)
    return x * lax.rsqrt(ms + NORM_EPS) * gain


def _in_proj_kernel(x_ref, g_ref, w_ref, o_ref, hn_ref):
    @pl.when(pl.program_id(1) == 0)
    def _():
        hn_ref[...] = _rms(x_ref[...], g_ref[...]).astype(BF16)

    o_ref[...] = jnp.dot(hn_ref[...], w_ref[...], preferred_element_type=F32).astype(o_ref.dtype)


def _in_proj(x, gain, w, layer, tm, tn):
    s, d = x.shape
    n = w.shape[-1]
    return pl.pallas_call(
        _in_proj_kernel,
        grid=(s // tm, n // tn),
        in_specs=[
            pl.BlockSpec((tm, d), lambda i, j: (i, 0)),
            pl.BlockSpec((1, d), lambda i, j: (0, 0)),
            pl.BlockSpec((None, d, tn), lambda i, j: (layer, 0, j)),
        ],
        out_specs=pl.BlockSpec((tm, tn), lambda i, j: (i, j)),
        out_shape=jax.ShapeDtypeStruct((s, n), BF16),
        scratch_shapes=[pltpu.VMEM((tm, d), BF16)],
        compiler_params=_cparams(("arbitrary", "arbitrary")),
        name="in_proj",
    )(x, gain, w)


def _rope(b, ct, st):
    return b * ct + pltpu.roll(b, KRA_X2_OFF, 1) * st


def _mla_prep_kernel(cq_ref, ckv_ref, kr_ref, ct_ref, st_ref, cost_ref, sint_ref, cqg_ref, ckvg_ref,
                     wuqt_ref, wkn_ref, wvt_ref, qgt_ref, kg_ref, qt_ref, k_ref, vt_ref):
    inv_head = 1.0 / QK_HEAD
    tm = cq_ref.shape[0]

    cqn_t = _rms(cq_ref[...].astype(F32), cqg_ref[...]).T.astype(BF16)
    q_t = jnp.dot(wuqt_ref[...], cqn_t, preferred_element_type=F32)
    qg_t = jnp.concatenate([qgt_ref[...]] * (tm // LANES), axis=1)
    cos_t = cost_ref[...]
    sin_t = sint_ref[...]
    zeros = jnp.zeros((ROPE_HALF, tm), F32)
    x1_rows = slice(NOPE_DIM, NOPE_DIM + ROPE_HALF)
    x2_rows = slice(NOPE_DIM + KRA_X2_OFF, NOPE_DIM + KRA_X2_OFF + ROPE_HALF)
    for h in range(MLA_HEADS):
        qh = q_t[h * HEAD_PAD:(h + 1) * HEAD_PAD, :]
        ssq = jnp.sum(qh * qh, axis=0, keepdims=True)
        qh = qh * lax.rsqrt(ssq * inv_head + NORM_EPS) * qg_t
        x1, x2 = qh[x1_rows, :], qh[x2_rows, :]
        qt_ref[h, :NOPE_DIM, :] = qh[:NOPE_DIM, :].astype(BF16)
        qt_ref[h, NOPE_DIM:, :] = jnp.concatenate(
            [x1 * cos_t - x2 * sin_t, zeros, x2 * cos_t + x1 * sin_t, zeros], axis=0).astype(BF16)

    ckvn = _rms(ckv_ref[...].astype(F32), ckvg_ref[...])
    v_t = jnp.dot(wvt_ref[...], ckvn.T.astype(BF16), preferred_element_type=F32)
    for h in range(MLA_HEADS):
        vt_ref[h, 0] = v_t[h * MLA_V_DIM:(h + 1) * MLA_V_DIM, :].astype(BF16)

    kn_all = jnp.dot(ckvn.astype(BF16), wkn_ref[...], preferred_element_type=F32)
    kg = kg_ref[...]
    kg_n = kg[:, :LANES]
    kg_r = kg[:, LANES:]
    kr = kr_ref[...].astype(F32)
    ssq_r = jnp.sum(kr * kr, axis=-1, keepdims=True)
    kr_rot = _rope(kr * kg_r, ct_ref[...], st_ref[...])
    for h in range(MLA_HEADS):
        kn = kn_all[:, h * NOPE_DIM:(h + 1) * NOPE_DIM]
        ssq = jnp.sum(kn * kn, axis=-1, keepdims=True) + ssq_r
        r = lax.rsqrt(ssq * inv_head + NORM_EPS)
        k_ref[:, h * HEAD_PAD:h * HEAD_PAD + LANES] = (kn * r * kg_n).astype(BF16)
        k_ref[:, h * HEAD_PAD + LANES:(h + 1) * HEAD_PAD] = (kr_rot * r).astype(BF16)


def _mla_prep(z, ct, st, cos_t, sin_t, cqg, ckvg, wuqt, wkn, wvt, qgt, kg, tm):
    s = z.shape[0]
    full = lambda a: pl.BlockSpec(a.shape, lambda i: (0, 0))
    return pl.pallas_call(
        _mla_prep_kernel,
        grid=(s // tm,),
        in_specs=[
            pl.BlockSpec((tm, Q_LORA), lambda i: (i, Z_CQ // Q_LORA)),
            pl.BlockSpec((tm, KV_LORA), lambda i: (i, Z_CKV // KV_LORA)),
            pl.BlockSpec((tm, LANES), lambda i: (i, Z_KR // LANES)),
            pl.BlockSpec((tm, LANES), lambda i: (i, 0)),
            pl.BlockSpec((tm, LANES), lambda i: (i, 0)),
            pl.BlockSpec((ROPE_HALF, tm), lambda i: (0, i)),
            pl.BlockSpec((ROPE_HALF, tm), lambda i: (0, i)),
            full(cqg), full(ckvg), full(wuqt), full(wkn), full(wvt), full(qgt), full(kg),
        ],
        out_specs=[
            pl.BlockSpec((MLA_HEADS, HEAD_PAD, tm), lambda i: (0, 0, i)),
            pl.BlockSpec((tm, MLA_HEADS * HEAD_PAD), lambda i: (i, 0)),
            pl.BlockSpec((MLA_HEADS, 1, MLA_V_DIM, tm), lambda i: (0, i, 0, 0)),
        ],
        out_shape=[
            jax.ShapeDtypeStruct((MLA_HEADS, HEAD_PAD, s), BF16),
            jax.ShapeDtypeStruct((s, MLA_HEADS * HEAD_PAD), BF16),
            jax.ShapeDtypeStruct((MLA_HEADS, s // tm, MLA_V_DIM, tm), BF16),
        ],
        compiler_params=_cparams(("arbitrary",)),
        name="mla_prep",
    )(z, z, z, ct, st, cos_t, sin_t, cqg, ckvg, wuqt, wkn, wvt, qgt, kg)


def _attn_kernel(qt_ref, k_ref, vt_ref, o_ref, sa_ref, sb_ref, m_ref, l_ref, acc_ref, *, tk):
    qi = pl.program_id(1)
    m_ref[...] = jnp.full(m_ref.shape, -1e30, F32)
    l_ref[...] = jnp.zeros(l_ref.shape, F32)
    acc_ref[...] = jnp.zeros(acc_ref.shape, F32)

    def scores(kj, s_ref, lo):
        start = pl.multiple_of(kj * tk, tk)
        s_ref[:, lo:] = jnp.dot(k_ref[pl.ds(start, tk), :], qt_ref[:, lo:],
                                preferred_element_type=F32)

    def update(kj, s_ref, lo, masked):
        s = s_ref[:, lo:]
        if masked:
            kc = lax.broadcasted_iota(jnp.int32, s.shape, 0) // CHUNK
            qc = lax.broadcasted_iota(jnp.int32, s.shape, 1) // CHUNK
            s = jnp.where(kc <= qc, s, -1e30)
        m_old = m_ref[:, lo:]
        m_new = jnp.maximum(m_old, jnp.max(s, axis=0, keepdims=True))
        alpha = jnp.exp2(m_old - m_new)
        p = jnp.exp2(s - m_new)
        l_ref[:, lo:] = alpha * l_ref[:, lo:] + jnp.sum(p, axis=0, keepdims=True)
        acc_ref[:, lo:] = alpha * acc_ref[:, lo:] + jnp.dot(vt_ref[kj], p.astype(BF16),
                                                            preferred_element_type=F32)
        m_ref[:, lo:] = m_new

    scores(0, sa_ref, 0)

    def body(i, carry):
        scores(2 * i + 1, sb_ref, 0)
        update(2 * i, sa_ref, 0, False)
        scores(2 * i + 2, sa_ref, 0)
        update(2 * i + 1, sb_ref, 0, False)
        return carry

    lax.fori_loop(0, qi, body, 0)
    scores(2 * qi + 1, sb_ref, tk)
    update(2 * qi, sa_ref, 0, True)
    update(2 * qi + 1, sb_ref, tk, True)
    o_ref[...] = (acc_ref[...] / l_ref[...]).T.astype(o_ref.dtype)


def _attn_shifted_kernel(qt_ref, k_ref, vt_ref, shift_ref, o_ref, l_ref, acc_ref, *, tk):
    qi = pl.program_id(1)
    l_ref[...] = jnp.zeros(l_ref.shape, F32)
    acc_ref[...] = jnp.zeros(acc_ref.shape, F32)

    def tile(kj, lo, masked):
        start = pl.multiple_of(kj * tk, tk)
        s = jnp.dot(k_ref[pl.ds(start, tk), :], qt_ref[:, lo:], preferred_element_type=F32)
        p = jnp.exp2(s - shift_ref[:, lo:])
        if masked:
            kc = lax.broadcasted_iota(jnp.int32, p.shape, 0) // CHUNK
            qc = lax.broadcasted_iota(jnp.int32, p.shape, 1) // CHUNK
            p = jnp.where(kc <= qc, p, 0.0)
        l_ref[:, lo:] += jnp.sum(p, axis=0, keepdims=True)
        acc_ref[:, lo:] += jnp.dot(vt_ref[kj], p.astype(BF16), preferred_element_type=F32)

    def body(i, carry):
        for u in range(4):
            tile(4 * i + u, 0, False)
        return carry

    lax.fori_loop(0, qi // 2, body, 0)

    @pl.when(qi % 2 == 1)
    def _():
        tile(2 * qi - 2, 0, False)
        tile(2 * qi - 1, 0, False)

    tile(2 * qi, 0, True)
    tile(2 * qi + 1, tk, True)
    o_ref[...] = (acc_ref[...] / l_ref[...]).T.astype(o_ref.dtype)


def _attention_shifted(qt, k, vt, shift, tk):
    s = k.shape[0]
    tq = 2 * tk
    return pl.pallas_call(
        functools.partial(_attn_shifted_kernel, tk=tk),
        grid=(MLA_HEADS, s // tq),
        in_specs=[
            pl.BlockSpec((None, HEAD_PAD, tq), lambda h, i: (h, 0, i)),
            pl.BlockSpec((s, HEAD_PAD), lambda h, i: (0, h)),
            pl.BlockSpec((None, s // tk, MLA_V_DIM, tk), lambda h, i: (h, 0, 0, 0)),
            pl.BlockSpec((1, tq), lambda h, i: (0, 0)),
        ],
        out_specs=pl.BlockSpec((tq, MLA_V_DIM), lambda h, i: (i, h)),
        out_shape=jax.ShapeDtypeStruct((s, MLA_WIDTH), BF16),
        scratch_shapes=[
            pltpu.VMEM((1, tq), F32),
            pltpu.VMEM((MLA_V_DIM, tq), F32),
        ],
        compiler_params=_cparams(("arbitrary", "arbitrary")),
        name="mla_attention_shifted",
    )(qt, k, vt, shift)


def _attention(qt, k, vt, tk):
    s = k.shape[0]
    tq = 2 * tk
    return pl.pallas_call(
        functools.partial(_attn_kernel, tk=tk),
        grid=(MLA_HEADS, s // tq),
        in_specs=[
            pl.BlockSpec((None, HEAD_PAD, tq), lambda h, i: (h, 0, i)),
            pl.BlockSpec((s, HEAD_PAD), lambda h, i: (0, h)),
            pl.BlockSpec((None, s // tk, MLA_V_DIM, tk), lambda h, i: (h, 0, 0, 0)),
        ],
        out_specs=pl.BlockSpec((tq, MLA_V_DIM), lambda h, i: (i, h)),
        out_shape=jax.ShapeDtypeStruct((s, MLA_WIDTH), BF16),
        scratch_shapes=[
            pltpu.VMEM((tk, tq), F32),
            pltpu.VMEM((tk, tq), F32),
            pltpu.VMEM((1, tq), F32),
            pltpu.VMEM((1, tq), F32),
            pltpu.VMEM((MLA_V_DIM, tq), F32),
        ],
        compiler_params=_cparams(("arbitrary", "arbitrary")),
        name="mla_attention",
    )(qt, k, vt)


def _gla_kernel(q_ref, k_ref, v_ref, g_ref, kra_ref, wg_ref, bg_ref, on_ref, tri_ref, o_ref,
                st_ref, *, tm):
    @pl.when(pl.program_id(0) == 0)
    def _():
        st_ref[...] = jnp.zeros(st_ref.shape, F32)

    pre = jnp.dot(kra_ref[...], wg_ref[...], preferred_element_type=F32) + bg_ref[...]
    log_a = (jnp.minimum(pre, 0.0) - jnp.log1p(jnp.exp(-jnp.abs(pre)))) * (1.0 / GATE_TAU)
    hi = log_a.astype(BF16)
    lo = (log_a - hi.astype(F32)).astype(BF16)
    tri = tri_ref[...]
    suffix = (jnp.dot(tri, hi, preferred_element_type=F32)
              + jnp.dot(tri, lo, preferred_element_type=F32))
    kd = k_ref[...].astype(F32) * jnp.exp(suffix)
    row = lax.broadcasted_iota(jnp.int32, kd.shape, 0)
    first_half = ((row // CHUNK) % 2) == 0
    kd_pair = (jnp.where(first_half, kd, 0.0).astype(BF16),
               jnp.where(first_half, 0.0, kd).astype(BF16))
    vt = v_ref[...].astype(F32).T.astype(BF16)
    on = on_ref[...]
    scale = GLA_DK ** -0.5

    n_chunks = tm // CHUNK
    kcols = [slice(h * GLA_DK, (h + 1) * GLA_DK) for h in range(GLA_HEADS)]
    vcols = [slice(h * GLA_DV, (h + 1) * GLA_DV) for h in range(GLA_HEADS)]

    uts = []
    for c in range(n_chunks):
        p0 = (c // 2) * 2 * CHUNK
        uts.append([jnp.dot(vt[vcols[h], p0:p0 + 2 * CHUNK],
                            kd_pair[c % 2][p0:p0 + 2 * CHUNK, kcols[h]],
                            preferred_element_type=F32) for h in range(GLA_HEADS)])

    cur = [st_ref[h] for h in range(GLA_HEADS)]
    states = []
    for c in range(n_chunks):
        r0 = c * CHUNK
        decay = jnp.exp(jnp.sum(log_a[r0:r0 + CHUNK, :], axis=0, keepdims=True))
        cur = [cur[h] * decay[:, kcols[h]] + uts[c][h] for h in range(GLA_HEADS)]
        states.append([s.astype(BF16) for s in cur])
    for h in range(GLA_HEADS):
        st_ref[h] = cur[h]

    for c in range(n_chunks):
        r0 = c * CHUNK
        for h in range(GLA_HEADS):
            o = lax.dot_general(q_ref[r0:r0 + CHUNK, kcols[h]], states[c][h],
                                (((1,), (1,)), ((), ())), preferred_element_type=F32) * scale
            o = _rms(o, on)
            gate = g_ref[r0:r0 + CHUNK, vcols[h]].astype(F32)
            o_ref[r0:r0 + CHUNK, vcols[h]] = (o * (gate * jax.nn.sigmoid(gate))).astype(o_ref.dtype)


def _gla(z, wg, bg, on, tri, tm):
    s = z.shape[0]
    full = lambda a: pl.BlockSpec(a.shape, lambda i: (0, 0))
    return pl.pallas_call(
        functools.partial(_gla_kernel, tm=tm),
        grid=(s // tm,),
        in_specs=[
            pl.BlockSpec((tm, GLA_K_WIDTH), lambda i: (i, Z_Q // GLA_K_WIDTH)),
            pl.BlockSpec((tm, GLA_K_WIDTH), lambda i: (i, Z_K // GLA_K_WIDTH)),
            pl.BlockSpec((tm, GLA_WIDTH), lambda i: (i, Z_V // GLA_WIDTH)),
            pl.BlockSpec((tm, GLA_WIDTH), lambda i: (i, Z_G // GLA_WIDTH)),
            pl.BlockSpec((tm, LANES), lambda i: (i, Z_A // LANES)),
            full(wg), full(bg), full(on), full(tri),
        ],
        out_specs=pl.BlockSpec((tm, GLA_WIDTH), lambda i: (i, 0)),
        out_shape=jax.ShapeDtypeStruct((s, GLA_WIDTH), BF16),
        scratch_shapes=[pltpu.VMEM((GLA_HEADS, GLA_DV, GLA_DK), F32)],
        compiler_params=_cparams(("arbitrary",)),
        name="gla",
    )(z, z, z, z, z, wg, bg, on, tri)


def _out_proj_kernel(a_ref, g_ref, x_ref, an_ref, fn_ref, w_ref, o_ref, hn_ref):
    an = _rms(a_ref[...].astype(F32), an_ref[...]).astype(BF16)
    acc = jnp.dot(an, w_ref[:MLA_WIDTH, :], preferred_element_type=F32)
    acc = acc + jnp.dot(g_ref[...], w_ref[MLA_WIDTH:, :], preferred_element_type=F32)
    x = x_ref[...] + acc
    o_ref[...] = x
    hn_ref[...] = _rms(x, fn_ref[...]).astype(BF16)


def _out_proj(a, g, x, an, fn, w, layer, tm):
    s, d = x.shape
    return pl.pallas_call(
        _out_proj_kernel,
        grid=(s // tm,),
        in_specs=[
            pl.BlockSpec((tm, MLA_WIDTH), lambda i: (i, 0)),
            pl.BlockSpec((tm, GLA_WIDTH), lambda i: (i, 0)),
            pl.BlockSpec((tm, d), lambda i: (i, 0)),
            pl.BlockSpec((1, MLA_WIDTH), lambda i: (0, 0)),
            pl.BlockSpec((1, d), lambda i: (0, 0)),
            pl.BlockSpec((None, MIX_WIDTH, d), lambda i: (layer, 0, 0)),
        ],
        out_specs=[pl.BlockSpec((tm, d), lambda i: (i, 0)), pl.BlockSpec((tm, d), lambda i: (i, 0))],
        out_shape=[jax.ShapeDtypeStruct((s, d), F32), jax.ShapeDtypeStruct((s, d), BF16)],
        compiler_params=_cparams(("arbitrary",)),
        name="out_proj",
    )(a, g, x, an, fn, w)


HALO = 8
FIX = 16


def _ffn_kernel(x_ref, hn_ref, wg_ref, wu_ref, cwg_ref, cwu_ref, cbg_ref, cbu_ref, wd_ref, o_ref,
                carry_ref, act_ref, *, tm):
    i = pl.program_id(0)
    f = pl.program_id(1)

    @pl.when(f == 0)
    def _():
        o_ref[...] = x_ref[...]

    @pl.when(i == 0)
    def _():
        carry_ref[f] = jnp.zeros(carry_ref.shape[1:], F32)

    hn = hn_ref[...]

    def conv(u, cw_ref, cb_ref, slot):
        w0, w1, w2 = cw_ref[0:1, :], cw_ref[1:2, :], cw_ref[2:3, :]
        cb = cb_ref[...]
        y = cb + w0 * pltpu.roll(u, 2, 0) + w1 * pltpu.roll(u, 1, 0) + w2 * u
        ext = jnp.concatenate([carry_ref[f, slot], u[:FIX, :]], axis=0)
        y_fix = (cb + w0 * pltpu.roll(ext, 2, 0)[HALO:, :] + w1 * pltpu.roll(ext, 1, 0)[HALO:, :]
                 + w2 * ext[HALO:, :])
        carry_ref[f, slot] = u[tm - HALO:, :]
        return y, y_fix

    ug = jnp.dot(hn, wg_ref[...], preferred_element_type=F32)
    yg, yg_fix = conv(ug, cwg_ref, cbg_ref, 0)
    uu = jnp.dot(hn, wu_ref[...], preferred_element_type=F32)
    yu, yu_fix = conv(uu, cwu_ref, cbu_ref, 1)
    act_ref[...] = (yg * jax.nn.sigmoid(yg) * yu).astype(BF16)
    act_ref[0:FIX, :] = (yg_fix * jax.nn.sigmoid(yg_fix) * yu_fix).astype(BF16)
    o_ref[...] += jnp.dot(act_ref[...], wd_ref[...], preferred_element_type=F32)


def _ffn(x, hn, w_up, conv_w, conv_b, w_down, layer, tm, tf):
    s, d = x.shape
    nf = D_FF // tf
    return pl.pallas_call(
        functools.partial(_ffn_kernel, tm=tm),
        grid=(s // tm, nf),
        in_specs=[
            pl.BlockSpec((tm, d), lambda i, f: (i, 0), pipeline_mode=pl.Buffered(1)),
            pl.BlockSpec((tm, d), lambda i, f: (i, 0)),
            pl.BlockSpec((None, d, tf), lambda i, f: (layer, 0, f)),
            pl.BlockSpec((None, d, tf), lambda i, f: (layer, 0, f + nf)),
            pl.BlockSpec((None, CONV_W, tf), lambda i, f: (layer, 0, f)),
            pl.BlockSpec((None, CONV_W, tf), lambda i, f: (layer, 0, f + nf)),
            pl.BlockSpec((None, 1, tf), lambda i, f: (layer, 0, f)),
            pl.BlockSpec((None, 1, tf), lambda i, f: (layer, 0, f + nf)),
            pl.BlockSpec((None, tf, d), lambda i, f: (layer, f, 0)),
        ],
        out_specs=pl.BlockSpec((tm, d), lambda i, f: (i, 0)),
        out_shape=jax.ShapeDtypeStruct((s, d), F32),
        scratch_shapes=[
            pltpu.VMEM((nf, 2, HALO, tf), F32),
            pltpu.VMEM((tm, tf), BF16),
        ],
        compiler_params=_cparams(("arbitrary", "arbitrary")),
        name="conv_ffn",
    )(x, hn, w_up, w_up, conv_w, conv_w, conv_b, conv_b, w_down)


def _pack_w_in(w):
    o_cq, o_ckv, o_kr = 0, Q_LORA, Q_LORA + KV_LORA
    o_q = o_kr + ROPE_DIM
    o_k = o_q + GLA_K_WIDTH
    o_v = o_k + GLA_K_WIDTH
    o_a = o_v + GLA_WIDTH
    o_g = o_a + GATE_RANK
    w = w.astype(BF16)
    zeros = lambda n: jnp.zeros(w.shape[:-1] + (n,), w.dtype)
    kr = jnp.concatenate([
        w[..., o_kr:o_kr + ROPE_HALF], zeros(KRA_X2_OFF - ROPE_HALF),
        w[..., o_kr + ROPE_HALF:o_kr + ROPE_DIM], zeros(LANES - KRA_X2_OFF - ROPE_HALF)], axis=-1)
    packed = jnp.concatenate([
        w[..., o_q:o_a], w[..., o_g:o_g + GLA_WIDTH], w[..., o_cq:o_kr], kr,
        w[..., o_a:o_a + GATE_RANK], zeros(LANES - GATE_RANK)], axis=-1)
    return packed.astype(BF16)


def _pack_head_vec(g):
    z = jnp.zeros((ROPE_HALF,), g.dtype)
    return jnp.concatenate([g[:NOPE_DIM], g[NOPE_DIM:NOPE_DIM + ROPE_HALF], z,
                            g[NOPE_DIM + ROPE_HALF:], z])[None, :]


def _pack_w_uq(w):
    r = w.shape[0]
    w = w.reshape(r, MLA_HEADS, QK_HEAD)
    z = jnp.zeros((r, MLA_HEADS, ROPE_HALF), w.dtype)
    w = jnp.concatenate([w[..., :NOPE_DIM], w[..., NOPE_DIM:NOPE_DIM + ROPE_HALF], z,
                         w[..., NOPE_DIM + ROPE_HALF:], z], axis=-1)
    return w.reshape(r, MLA_HEADS * HEAD_PAD).astype(BF16)


def _pack_w_ukv(w):
    r = w.shape[0]
    w = w.reshape(r, MLA_HEADS, NOPE_DIM + MLA_V_DIM)
    return (w[..., :NOPE_DIM].reshape(r, -1).astype(BF16),
            w[..., NOPE_DIM:].reshape(r, -1).T.astype(BF16))


def _pack_w_gate(w):
    z = jnp.zeros((LANES - GATE_RANK, w.shape[1]), w.dtype)
    return jnp.concatenate([w, z], axis=0).astype(BF16)


def _rope_tables(positions):
    inv_freq = ROPE_BASE ** (-jnp.arange(0, ROPE_DIM, 2, dtype=F32) / ROPE_DIM)
    angle = positions.astype(F32)[:, None] * inv_freq
    cos, sin = jnp.cos(angle), jnp.sin(angle)
    z = jnp.zeros_like(cos)
    return (jnp.concatenate([cos, z, cos, z], axis=1), jnp.concatenate([-sin, z, sin, z], axis=1),
            cos.T, sin.T)


def _suffix_matrix(tm):
    t = jnp.arange(tm)
    same = (t[:, None] // CHUNK) == (t[None, :] // CHUNK)
    return (same & (t[None, :] > t[:, None])).astype(BF16)


def kernel(x, positions, attn_norm, w_in, cq_norm, w_uq, ckv_norm, w_ukv, q_norm, k_norm, w_gate,
           b_gate, gla_out_norm, mla_out_norm, w_out, ffn_norm, w_up, conv_w, conv_b, w_down):
    batch, seq, d = x.shape
    assert batch == 1 and d == D_MODEL
    tm_proj = min(1024, seq)
    tm = min(512, seq)
    tm_gla = min(512, seq)
    ct, st, cos_t, sin_t = _rope_tables(positions[0])
    tri = _suffix_matrix(tm_gla)
    q_scale = math.log2(math.e) / math.sqrt(QK_HEAD)
    w_out_b, w_up_b, w_down_b = w_out.astype(BF16), w_up.astype(BF16), w_down.astype(BF16)
    w_in_b = _pack_w_in(w_in)
    h = x[0]
    for l in range(DEPTH):
        z = _in_proj(h, attn_norm[l][None, :], w_in_b, l, tm_proj, 2048)
        qg = _pack_head_vec(q_norm[l]) * q_scale
        kg = _pack_head_vec(k_norm[l])
        w_kn, w_vt = _pack_w_ukv(w_ukv[l])
        q, k, v = _mla_prep(z, ct, st, cos_t, sin_t, cq_norm[l][None, :], ckv_norm[l][None, :],
                            _pack_w_uq(w_uq[l]).T, w_kn, w_vt,
                            jnp.broadcast_to(qg.T, (HEAD_PAD, LANES)), kg, tm)
        bound = SCORE_BOUND_MARGIN * QK_HEAD * jnp.max(jnp.abs(qg)) * jnp.max(jnp.abs(kg))
        a = lax.cond(
            bound <= SOFTMAX_SHIFT_LIMIT,
            lambda: _attention_shifted(q, k, v, jnp.full((1, 2 * tm), bound, F32), tm),
            lambda: _attention(q, k, v, tm))
        g = _gla(z, _pack_w_gate(w_gate[l]), b_gate[l][None, :], gla_out_norm[l][None, :], tri,
                 tm_gla)
        h, hn = _out_proj(a, g, h, mla_out_norm[l][None, :], ffn_norm[l][None, :], w_out_b, l, tm)
        h = _ffn(h, hn, w_up_b, conv_w, conv_b[:, None, :], w_down_b, l, tm_proj, 512)
    return h[None]
```

```python
import functools
import math

import jax
import jax.numpy as jnp
from jax import lax
from jax.experimental import pallas as pl
from jax.experimental.pallas import tpu as pltpu

F32 = jnp.float32
BF16 = jnp.bfloat16

D_MODEL = 2048
DEPTH = 2
CHUNK = 64
MLA_HEADS = 8
Q_LORA = 512
KV_LORA = 256
NOPE_DIM = 128
ROPE_DIM = 64
ROPE_HALF = ROPE_DIM // 2
QK_HEAD = NOPE_DIM + ROPE_DIM
MLA_V_DIM = 128
MLA_WIDTH = MLA_HEADS * MLA_V_DIM
ROPE_BASE = 10000.0
GLA_HEADS = 4
GLA_DK = 128
GLA_DV = 256
GLA_K_WIDTH = GLA_HEADS * GLA_DK
GLA_WIDTH = GLA_HEADS * GLA_DV
GATE_RANK = 16
GATE_TAU = 16.0
MIX_WIDTH = MLA_WIDTH + GLA_WIDTH
D_FF = 5632
CONV_W = 3
NORM_EPS = 1e-6

LANES = 128
HEAD_PAD = 2 * LANES

Z_Q, Z_K, Z_V, Z_G, Z_CQ, Z_CKV, Z_KR, Z_A = 0, 512, 1024, 2048, 3072, 3584, 3840, 3968
Z_WIDTH = 4096
KRA_X2_OFF = 2 * ROPE_HALF

VMEM_LIMIT = 56 * 1024 * 1024

SOFTMAX_SHIFT_LIMIT = 60.0
SCORE_BOUND_MARGIN = 1.02


def _cparams(sem):
    return pltpu.CompilerParams(dimension_semantics=sem, vmem_limit_bytes=VMEM_LIMIT)


def _rms(x, gain):
    ms = jnp.mean(x * x, axis=-1, keepdims=True)
    return x * lax.rsqrt(ms + NORM_EPS) * gain


def _in_proj_kernel(x_ref, g_ref, w_ref, o_ref, hn_ref):
    @pl.when(pl.program_id(1) == 0)
    def _():
        hn_ref[...] = _rms(x_ref[...], g_ref[...]).astype(BF16)

    o_ref[...] = jnp.dot(hn_ref[...], w_ref[...], preferred_element_type=F32).astype(o_ref.dtype)


def _in_proj(x, gain, w, layer, tm, tn):
    s, d = x.shape
    n = w.shape[-1]
    return pl.pallas_call(
        _in_proj_kernel,
        grid=(s // tm, n // tn),
        in_specs=[
            pl.BlockSpec((tm, d), lambda i, j: (i, 0)),
            pl.BlockSpec((1, d), lambda i, j: (0, 0)),
            pl.BlockSpec((None, d, tn), lambda i, j: (layer, 0, j)),
        ],
        out_specs=pl.BlockSpec((tm, tn), lambda i, j: (i, j)),
        out_shape=jax.ShapeDtypeStruct((s, n), BF16),
        scratch_shapes=[pltpu.VMEM((tm, d), BF16)],
        compiler_params=_cparams(("arbitrary", "arbitrary")),
        name="in_proj",
    )(x, gain, w)


def _rope(b, ct, st):
    return b * ct + pltpu.roll(b, KRA_X2_OFF, 1) * st


def _mla_prep_kernel(cq_ref, ckv_ref, kr_ref, ct_ref, st_ref, cost_ref, sint_ref, cqg_ref, ckvg_ref,
                     wuqt_ref, wkn_ref, wvt_ref, qgt_ref, kg_ref, qt_ref, k_ref, vt_ref):
    inv_head = 1.0 / QK_HEAD
    tm = cq_ref.shape[0]

    cqn_t = _rms(cq_ref[...].astype(F32), cqg_ref[...]).T.astype(BF16)
    q_t = jnp.dot(wuqt_ref[...], cqn_t, preferred_element_type=F32)
    qg_t = jnp.concatenate([qgt_ref[...]] * (tm // LANES), axis=1)
    cos_t = cost_ref[...]
    sin_t = sint_ref[...]
    zeros = jnp.zeros((ROPE_HALF, tm), F32)
    x1_rows = slice(NOPE_DIM, NOPE_DIM + ROPE_HALF)
    x2_rows = slice(NOPE_DIM + KRA_X2_OFF, NOPE_DIM + KRA_X2_OFF + ROPE_HALF)
    for h in range(MLA_HEADS):
        qh = q_t[h * HEAD_PAD:(h + 1) * HEAD_PAD, :]
        ssq = jnp.sum(qh * qh, axis=0, keepdims=True)
        qh = qh * lax.rsqrt(ssq * inv_head + NORM_EPS) * qg_t
        x1, x2 = qh[x1_rows, :], qh[x2_rows, :]
        qt_ref[h, :NOPE_DIM, :] = qh[:NOPE_DIM, :].astype(BF16)
        qt_ref[h, NOPE_DIM:, :] = jnp.concatenate(
            [x1 * cos_t - x2 * sin_t, zeros, x2 * cos_t + x1 * sin_t, zeros], axis=0).astype(BF16)

    ckvn = _rms(ckv_ref[...].astype(F32), ckvg_ref[...])
    v_t = jnp.dot(wvt_ref[...], ckvn.T.astype(BF16), preferred_element_type=F32)
    for h in range(MLA_HEADS):
        vt_ref[h, 0] = v_t[h * MLA_V_DIM:(h + 1) * MLA_V_DIM, :].astype(BF16)

    kn_all = jnp.dot(ckvn.astype(BF16), wkn_ref[...], preferred_element_type=F32)
    kg = kg_ref[...]
    kg_n = kg[:, :LANES]
    kg_r = kg[:, LANES:]
    kr = kr_ref[...].astype(F32)
    ssq_r = jnp.sum(kr * kr, axis=-1, keepdims=True)
    kr_rot = _rope(kr * kg_r, ct_ref[...], st_ref[...])
    for h in range(MLA_HEADS):
        kn = kn_all[:, h * NOPE_DIM:(h + 1) * NOPE_DIM]
        ssq = jnp.sum(kn * kn, axis=-1, keepdims=True) + ssq_r
        r = lax.rsqrt(ssq * inv_head + NORM_EPS)
        k_ref[:, h * HEAD_PAD:h * HEAD_PAD + LANES] = (kn * r * kg_n).astype(BF16)
        k_ref[:, h * HEAD_PAD + LANES:(h + 1) * HEAD_PAD] = (kr_rot * r).astype(BF16)


def _mla_prep(z, ct, st, cos_t, sin_t, cqg, ckvg, wuqt, wkn, wvt, qgt, kg, tm):
    s = z.shape[0]
    full = lambda a: pl.BlockSpec(a.shape, lambda i: (0, 0))
    return pl.pallas_call(
        _mla_prep_kernel,
        grid=(s // tm,),
        in_specs=[
            pl.BlockSpec((tm, Q_LORA), lambda i: (i, Z_CQ // Q_LORA)),
            pl.BlockSpec((tm, KV_LORA), lambda i: (i, Z_CKV // KV_LORA)),
            pl.BlockSpec((tm, LANES), lambda i: (i, Z_KR // LANES)),
            pl.BlockSpec((tm, LANES), lambda i: (i, 0)),
            pl.BlockSpec((tm, LANES), lambda i: (i, 0)),
            pl.BlockSpec((ROPE_HALF, tm), lambda i: (0, i)),
            pl.BlockSpec((ROPE_HALF, tm), lambda i: (0, i)),
            full(cqg), full(ckvg), full(wuqt), full(wkn), full(wvt), full(qgt), full(kg),
        ],
        out_specs=[
            pl.BlockSpec((MLA_HEADS, HEAD_PAD, tm), lambda i: (0, 0, i)),
            pl.BlockSpec((tm, MLA_HEADS * HEAD_PAD), lambda i: (i, 0)),
            pl.BlockSpec((MLA_HEADS, 1, MLA_V_DIM, tm), lambda i: (0, i, 0, 0)),
        ],
        out_shape=[
            jax.ShapeDtypeStruct((MLA_HEADS, HEAD_PAD, s), BF16),
            jax.ShapeDtypeStruct((s, MLA_HEADS * HEAD_PAD), BF16),
            jax.ShapeDtypeStruct((MLA_HEADS, s // tm, MLA_V_DIM, tm), BF16),
        ],
        compiler_params=_cparams(("arbitrary",)),
        name="mla_prep",
    )(z, z, z, ct, st, cos_t, sin_t, cqg, ckvg, wuqt, wkn, wvt, qgt, kg)


def _attn_kernel(qt_ref, k_ref, vt_ref, o_ref, sa_ref, sb_ref, m_ref, l_ref, acc_ref, *, tk):
    qi = pl.program_id(1)
    m_ref[...] = jnp.full(m_ref.shape, -1e30, F32)
    l_ref[...] = jnp.zeros(l_ref.shape, F32)
    acc_ref[...] = jnp.zeros(acc_ref.shape, F32)

    def scores(kj, s_ref, lo):
        start = pl.multiple_of(kj * tk, tk)
        s_ref[:, lo:] = jnp.dot(k_ref[pl.ds(start, tk), :], qt_ref[:, lo:],
                                preferred_element_type=F32)

    def update(kj, s_ref, lo, masked):
        s = s_ref[:, lo:]
        if masked:
            kc = lax.broadcasted_iota(jnp.int32, s.shape, 0) // CHUNK
            qc = lax.broadcasted_iota(jnp.int32, s.shape, 1) // CHUNK
            s = jnp.where(kc <= qc, s, -1e30)
        m_old = m_ref[:, lo:]
        m_new = jnp.maximum(m_old, jnp.max(s, axis=0, keepdims=True))
        alpha = jnp.exp2(m_old - m_new)
        p = jnp.exp2(s - m_new)
        l_ref[:, lo:] = alpha * l_ref[:, lo:] + jnp.sum(p, axis=0, keepdims=True)
        acc_ref[:, lo:] = alpha * acc_ref[:, lo:] + jnp.dot(vt_ref[kj], p.astype(BF16),
                                                            preferred_element_type=F32)
        m_ref[:, lo:] = m_new

    scores(0, sa_ref, 0)

    def body(i, carry):
        scores(2 * i + 1, sb_ref, 0)
        update(2 * i, sa_ref, 0, False)
        scores(2 * i + 2, sa_ref, 0)
        update(2 * i + 1, sb_ref, 0, False)
        return carry

    lax.fori_loop(0, qi, body, 0)
    scores(2 * qi + 1, sb_ref, tk)
    update(2 * qi, sa_ref, 0, True)
    update(2 * qi + 1, sb_ref, tk, True)
    o_ref[...] = (acc_ref[...] / l_ref[...]).T.astype(o_ref.dtype)


def _attn_shifted_kernel(qt_ref, k_ref, vt_ref, shift_ref, o_ref, l_ref, acc_ref, *, tk, nq):
    qi = pl.program_id(1)
    l_ref[...] = jnp.zeros(l_ref.shape, F32)
    acc_ref[...] = jnp.zeros(acc_ref.shape, F32)

    def tile(kj, lo, masked):
        start = pl.multiple_of(kj * tk, tk)
        s = jnp.dot(k_ref[pl.ds(start, tk), :], qt_ref[:, lo:], preferred_element_type=F32)
        p = jnp.exp2(s - shift_ref[:, lo:])
        if masked:
            kc = lax.broadcasted_iota(jnp.int32, p.shape, 0) // CHUNK
            qc = lax.broadcasted_iota(jnp.int32, p.shape, 1) // CHUNK
            p = jnp.where(kc <= qc, p, 0.0)
        l_ref[:, lo:] += jnp.sum(p, axis=0, keepdims=True)
        acc_ref[:, lo:] += jnp.dot(vt_ref[kj], p.astype(BF16), preferred_element_type=F32)

    def body(i, carry):
        for u in range(nq):
            tile(nq * i + u, 0, False)
        return carry

    lax.fori_loop(0, qi, body, 0)
    for u in range(nq):
        tile(nq * qi + u, u * tk, True)
    o_ref[...] = (acc_ref[...] / l_ref[...]).T.astype(o_ref.dtype)


def _attention_shifted(qt, k, vt, shift, tk, nq):
    s = k.shape[0]
    tq = nq * tk
    return pl.pallas_call(
        functools.partial(_attn_shifted_kernel, tk=tk, nq=nq),
        grid=(MLA_HEADS, s // tq),
        in_specs=[
            pl.BlockSpec((None, HEAD_PAD, tq), lambda h, i: (h, 0, i)),
            pl.BlockSpec((s, HEAD_PAD), lambda h, i: (0, h)),
            pl.BlockSpec((None, s // tk, MLA_V_DIM, tk), lambda h, i: (h, 0, 0, 0)),
            pl.BlockSpec((1, tq), lambda h, i: (0, 0)),
        ],
        out_specs=pl.BlockSpec((tq, MLA_V_DIM), lambda h, i: (i, h)),
        out_shape=jax.ShapeDtypeStruct((s, MLA_WIDTH), BF16),
        scratch_shapes=[
            pltpu.VMEM((1, tq), F32),
            pltpu.VMEM((MLA_V_DIM, tq), F32),
        ],
        compiler_params=_cparams(("arbitrary", "arbitrary")),
        name="mla_attention_shifted",
    )(qt, k, vt, shift)


def _attention(qt, k, vt, tk):
    s = k.shape[0]
    tq = 2 * tk
    return pl.pallas_call(
        functools.partial(_attn_kernel, tk=tk),
        grid=(MLA_HEADS, s // tq),
        in_specs=[
            pl.BlockSpec((None, HEAD_PAD, tq), lambda h, i: (h, 0, i)),
            pl.BlockSpec((s, HEAD_PAD), lambda h, i: (0, h)),
            pl.BlockSpec((None, s // tk, MLA_V_DIM, tk), lambda h, i: (h, 0, 0, 0)),
        ],
        out_specs=pl.BlockSpec((tq, MLA_V_DIM), lambda h, i: (i, h)),
        out_shape=jax.ShapeDtypeStruct((s, MLA_WIDTH), BF16),
        scratch_shapes=[
            pltpu.VMEM((tk, tq), F32),
            pltpu.VMEM((tk, tq), F32),
            pltpu.VMEM((1, tq), F32),
            pltpu.VMEM((1, tq), F32),
            pltpu.VMEM((MLA_V_DIM, tq), F32),
        ],
        compiler_params=_cparams(("arbitrary", "arbitrary")),
        name="mla_attention",
    )(qt, k, vt)


def _gla_kernel(q_ref, k_ref, v_ref, g_ref, kra_ref, wg_ref, bg_ref, on_ref, tri_ref, o_ref,
                st_ref, *, tm):
    @pl.when(pl.program_id(0) == 0)
    def _():
        st_ref[...] = jnp.zeros(st_ref.shape, F32)

    pre = jnp.dot(kra_ref[...], wg_ref[...], preferred_element_type=F32) + bg_ref[...]
    log_a = (jnp.minimum(pre, 0.0) - jnp.log1p(jnp.exp(-jnp.abs(pre)))) * (1.0 / GATE_TAU)
    hi = log_a.astype(BF16)
    lo = (log_a - hi.astype(F32)).astype(BF16)
    tri = tri_ref[...]
    suffix = (jnp.dot(tri, hi, preferred_element_type=F32)
              + jnp.dot(tri, lo, preferred_element_type=F32))
    kd = k_ref[...].astype(F32) * jnp.exp(suffix)
    row = lax.broadcasted_iota(jnp.int32, kd.shape, 0)
    first_half = ((row // CHUNK) % 2) == 0
    kd_pair = (jnp.where(first_half, kd, 0.0).astype(BF16),
               jnp.where(first_half, 0.0, kd).astype(BF16))
    vt = v_ref[...].astype(F32).T.astype(BF16)
    on = on_ref[...]
    scale = GLA_DK ** -0.5

    n_chunks = tm // CHUNK
    kcols = [slice(h * GLA_DK, (h + 1) * GLA_DK) for h in range(GLA_HEADS)]
    vcols = [slice(h * GLA_DV, (h + 1) * GLA_DV) for h in range(GLA_HEADS)]

    uts = []
    for c in range(n_chunks):
        p0 = (c // 2) * 2 * CHUNK
        uts.append([jnp.dot(vt[vcols[h], p0:p0 + 2 * CHUNK],
                            kd_pair[c % 2][p0:p0 + 2 * CHUNK, kcols[h]],
                            preferred_element_type=F32) for h in range(GLA_HEADS)])

    cur = [st_ref[h] for h in range(GLA_HEADS)]
    states = []
    for c in range(n_chunks):
        r0 = c * CHUNK
        decay = jnp.exp(jnp.sum(log_a[r0:r0 + CHUNK, :], axis=0, keepdims=True))
        cur = [cur[h] * decay[:, kcols[h]] + uts[c][h] for h in range(GLA_HEADS)]
        states.append([s.astype(BF16) for s in cur])
    for h in range(GLA_HEADS):
        st_ref[h] = cur[h]

    for c in range(n_chunks):
        r0 = c * CHUNK
        for h in range(GLA_HEADS):
            o = lax.dot_general(q_ref[r0:r0 + CHUNK, kcols[h]], states[c][h],
                                (((1,), (1,)), ((), ())), preferred_element_type=F32) * scale
            o = _rms(o, on)
            gate = g_ref[r0:r0 + CHUNK, vcols[h]].astype(F32)
            o_ref[r0:r0 + CHUNK, vcols[h]] = (o * (gate * jax.nn.sigmoid(gate))).astype(o_ref.dtype)


def _gla(z, wg, bg, on, tri, tm):
    s = z.shape[0]
    full = lambda a: pl.BlockSpec(a.shape, lambda i: (0, 0))
    return pl.pallas_call(
        functools.partial(_gla_kernel, tm=tm),
        grid=(s // tm,),
        in_specs=[
            pl.BlockSpec((tm, GLA_K_WIDTH), lambda i: (i, Z_Q // GLA_K_WIDTH)),
            pl.BlockSpec((tm, GLA_K_WIDTH), lambda i: (i, Z_K // GLA_K_WIDTH)),
            pl.BlockSpec((tm, GLA_WIDTH), lambda i: (i, Z_V // GLA_WIDTH)),
            pl.BlockSpec((tm, GLA_WIDTH), lambda i: (i, Z_G // GLA_WIDTH)),
            pl.BlockSpec((tm, LANES), lambda i: (i, Z_A // LANES)),
            full(wg), full(bg), full(on), full(tri),
        ],
        out_specs=pl.BlockSpec((tm, GLA_WIDTH), lambda i: (i, 0)),
        out_shape=jax.ShapeDtypeStruct((s, GLA_WIDTH), BF16),
        scratch_shapes=[pltpu.VMEM((GLA_HEADS, GLA_DV, GLA_DK), F32)],
        compiler_params=_cparams(("arbitrary",)),
        name="gla",
    )(z, z, z, z, z, wg, bg, on, tri)


def _out_proj_kernel(a_ref, g_ref, x_ref, an_ref, fn_ref, w_ref, o_ref, hn_ref):
    an = _rms(a_ref[...].astype(F32), an_ref[...]).astype(BF16)
    acc = jnp.dot(an, w_ref[:MLA_WIDTH, :], preferred_element_type=F32)
    acc = acc + jnp.dot(g_ref[...], w_ref[MLA_WIDTH:, :], preferred_element_type=F32)
    x = x_ref[...] + acc
    o_ref[...] = x
    hn_ref[...] = _rms(x, fn_ref[...]).astype(BF16)


def _out_proj(a, g, x, an, fn, w, layer, tm):
    s, d = x.shape
    return pl.pallas_call(
        _out_proj_kernel,
        grid=(s // tm,),
        in_specs=[
            pl.BlockSpec((tm, MLA_WIDTH), lambda i: (i, 0)),
            pl.BlockSpec((tm, GLA_WIDTH), lambda i: (i, 0)),
            pl.BlockSpec((tm, d), lambda i: (i, 0)),
            pl.BlockSpec((1, MLA_WIDTH), lambda i: (0, 0)),
            pl.BlockSpec((1, d), lambda i: (0, 0)),
            pl.BlockSpec((None, MIX_WIDTH, d), lambda i: (layer, 0, 0)),
        ],
        out_specs=[pl.BlockSpec((tm, d), lambda i: (i, 0)), pl.BlockSpec((tm, d), lambda i: (i, 0))],
        out_shape=[jax.ShapeDtypeStruct((s, d), F32), jax.ShapeDtypeStruct((s, d), BF16)],
        compiler_params=_cparams(("arbitrary",)),
        name="out_proj",
    )(a, g, x, an, fn, w)


HALO = 8
FIX = 16


def _ffn_kernel(x_ref, hn_ref, wg_ref, wu_ref, cwg_ref, cwu_ref, cbg_ref, cbu_ref, wd_ref, o_ref,
                carry_ref, act_ref, *, tm):
    i = pl.program_id(0)
    f = pl.program_id(1)

    @pl.when(f == 0)
    def _():
        o_ref[...] = x_ref[...]

    @pl.when(i == 0)
    def _():
        carry_ref[f] = jnp.zeros(carry_ref.shape[1:], F32)

    hn = hn_ref[...]

    def conv(u, cw_ref, cb_ref, slot):
        w0, w1, w2 = cw_ref[0:1, :], cw_ref[1:2, :], cw_ref[2:3, :]
        cb = cb_ref[...]
        y = cb + w0 * pltpu.roll(u, 2, 0) + w1 * pltpu.roll(u, 1, 0) + w2 * u
        ext = jnp.concatenate([carry_ref[f, slot], u[:FIX, :]], axis=0)
        y_fix = (cb + w0 * pltpu.roll(ext, 2, 0)[HALO:, :] + w1 * pltpu.roll(ext, 1, 0)[HALO:, :]
                 + w2 * ext[HALO:, :])
        carry_ref[f, slot] = u[tm - HALO:, :]
        return y, y_fix

    ug = jnp.dot(hn, wg_ref[...], preferred_element_type=F32)
    yg, yg_fix = conv(ug, cwg_ref, cbg_ref, 0)
    uu = jnp.dot(hn, wu_ref[...], preferred_element_type=F32)
    yu, yu_fix = conv(uu, cwu_ref, cbu_ref, 1)
    act_ref[...] = (yg * jax.nn.sigmoid(yg) * yu).astype(BF16)
    act_ref[0:FIX, :] = (yg_fix * jax.nn.sigmoid(yg_fix) * yu_fix).astype(BF16)
    o_ref[...] += jnp.dot(act_ref[...], wd_ref[...], preferred_element_type=F32)


def _ffn(x, hn, w_up, conv_w, conv_b, w_down, layer, tm, tf):
    s, d = x.shape
    nf = D_FF // tf
    return pl.pallas_call(
        functools.partial(_ffn_kernel, tm=tm),
        grid=(s // tm, nf),
        in_specs=[
            pl.BlockSpec((tm, d), lambda i, f: (i, 0), pipeline_mode=pl.Buffered(1)),
            pl.BlockSpec((tm, d), lambda i, f: (i, 0)),
            pl.BlockSpec((None, d, tf), lambda i, f: (layer, 0, f)),
            pl.BlockSpec((None, d, tf), lambda i, f: (layer, 0, f + nf)),
            pl.BlockSpec((None, CONV_W, tf), lambda i, f: (layer, 0, f)),
            pl.BlockSpec((None, CONV_W, tf), lambda i, f: (layer, 0, f + nf)),
            pl.BlockSpec((None, 1, tf), lambda i, f: (layer, 0, f)),
            pl.BlockSpec((None, 1, tf), lambda i, f: (layer, 0, f + nf)),
            pl.BlockSpec((None, tf, d), lambda i, f: (layer, f, 0)),
        ],
        out_specs=pl.BlockSpec((tm, d), lambda i, f: (i, 0)),
        out_shape=jax.ShapeDtypeStruct((s, d), F32),
        scratch_shapes=[
            pltpu.VMEM((nf, 2, HALO, tf), F32),
            pltpu.VMEM((tm, tf), BF16),
        ],
        compiler_params=_cparams(("arbitrary", "arbitrary")),
        name="conv_ffn",
    )(x, hn, w_up, w_up, conv_w, conv_w, conv_b, conv_b, w_down)


def _pack_w_in(w):
    o_cq, o_ckv, o_kr = 0, Q_LORA, Q_LORA + KV_LORA
    o_q = o_kr + ROPE_DIM
    o_k = o_q + GLA_K_WIDTH
    o_v = o_k + GLA_K_WIDTH
    o_a = o_v + GLA_WIDTH
    o_g = o_a + GATE_RANK
    w = w.astype(BF16)
    zeros = lambda n: jnp.zeros(w.shape[:-1] + (n,), w.dtype)
    kr = jnp.concatenate([
        w[..., o_kr:o_kr + ROPE_HALF], zeros(KRA_X2_OFF - ROPE_HALF),
        w[..., o_kr + ROPE_HALF:o_kr + ROPE_DIM], zeros(LANES - KRA_X2_OFF - ROPE_HALF)], axis=-1)
    packed = jnp.concatenate([
        w[..., o_q:o_a], w[..., o_g:o_g + GLA_WIDTH], w[..., o_cq:o_kr], kr,
        w[..., o_a:o_a + GATE_RANK], zeros(LANES - GATE_RANK)], axis=-1)
    return packed.astype(BF16)


def _pack_head_vec(g):
    z = jnp.zeros((ROPE_HALF,), g.dtype)
    return jnp.concatenate([g[:NOPE_DIM], g[NOPE_DIM:NOPE_DIM + ROPE_HALF], z,
                            g[NOPE_DIM + ROPE_HALF:], z])[None, :]


def _pack_w_uq(w):
    r = w.shape[0]
    w = w.reshape(r, MLA_HEADS, QK_HEAD)
    z = jnp.zeros((r, MLA_HEADS, ROPE_HALF), w.dtype)
    w = jnp.concatenate([w[..., :NOPE_DIM], w[..., NOPE_DIM:NOPE_DIM + ROPE_HALF], z,
                         w[..., NOPE_DIM + ROPE_HALF:], z], axis=-1)
    return w.reshape(r, MLA_HEADS * HEAD_PAD).astype(BF16)


def _pack_w_ukv(w):
    r = w.shape[0]
    w = w.reshape(r, MLA_HEADS, NOPE_DIM + MLA_V_DIM)
    return (w[..., :NOPE_DIM].reshape(r, -1).astype(BF16),
            w[..., NOPE_DIM:].reshape(r, -1).T.astype(BF16))


def _pack_w_gate(w):
    z = jnp.zeros((LANES - GATE_RANK, w.shape[1]), w.dtype)
    return jnp.concatenate([w, z], axis=0).astype(BF16)


def _rope_tables(positions):
    inv_freq = ROPE_BASE ** (-jnp.arange(0, ROPE_DIM, 2, dtype=F32) / ROPE_DIM)
    angle = positions.astype(F32)[:, None] * inv_freq
    cos, sin = jnp.cos(angle), jnp.sin(angle)
    z = jnp.zeros_like(cos)
    return (jnp.concatenate([cos, z, cos, z], axis=1), jnp.concatenate([-sin, z, sin, z], axis=1),
            cos.T, sin.T)


def _suffix_matrix(tm):
    t = jnp.arange(tm)
    same = (t[:, None] // CHUNK) == (t[None, :] // CHUNK)
    return (same & (t[None, :] > t[:, None])).astype(BF16)


def kernel(x, positions, attn_norm, w_in, cq_norm, w_uq, ckv_norm, w_ukv, q_norm, k_norm, w_gate,
           b_gate, gla_out_norm, mla_out_norm, w_out, ffn_norm, w_up, conv_w, conv_b, w_down):
    batch, seq, d = x.shape
    assert batch == 1 and d == D_MODEL
    tm_proj = min(1024, seq)
    tm = min(512, seq)
    tm_gla = min(512, seq)
    nq = min(8, seq // tm)
    ct, st, cos_t, sin_t = _rope_tables(positions[0])
    tri = _suffix_matrix(tm_gla)
    q_scale = math.log2(math.e) / math.sqrt(QK_HEAD)
    w_out_b, w_up_b, w_down_b = w_out.astype(BF16), w_up.astype(BF16), w_down.astype(BF16)
    w_in_b = _pack_w_in(w_in)
    h = x[0]
    for l in range(DEPTH):
        z = _in_proj(h, attn_norm[l][None, :], w_in_b, l, tm_proj, 2048)
        qg = _pack_head_vec(q_norm[l]) * q_scale
        kg = _pack_head_vec(k_norm[l])
        w_kn, w_vt = _pack_w_ukv(w_ukv[l])
        q, k, v = _mla_prep(z, ct, st, cos_t, sin_t, cq_norm[l][None, :], ckv_norm[l][None, :],
                            _pack_w_uq(w_uq[l]).T, w_kn, w_vt,
                            jnp.broadcast_to(qg.T, (HEAD_PAD, LANES)), kg, tm)
        bound = SCORE_BOUND_MARGIN * QK_HEAD * jnp.max(jnp.abs(qg)) * jnp.max(jnp.abs(kg))
        a = lax.cond(
            bound <= SOFTMAX_SHIFT_LIMIT,
            lambda: _attention_shifted(q, k, v, jnp.full((1, nq * tm), bound, F32), tm, nq),
            lambda: _attention(q, k, v, tm))
        g = _gla(z, _pack_w_gate(w_gate[l]), b_gate[l][None, :], gla_out_norm[l][None, :], tri,
                 tm_gla)
        h, hn = _out_proj(a, g, h, mla_out_norm[l][None, :], ffn_norm[l][None, :], w_out_b, l, tm)
        h = _ffn(h, hn, w_up_b, conv_w, conv_b[:, None, :], w_down_b, l, tm_proj, 512)
    return h[None]
```
